```python
import jax, jax.numpy as jnp
from jax import lax
import numpy as np


D_MODEL = 1024
BATCH = 8
SEQ = 2048
DEPTH = 2
DEC_BATCH = 128
DEC_SEQ = 1
PAST_LEN = 16384
PAGE_SIZE = 128

RET_HEADS = 4
RET_DK = 256
RET_DV = 512
RET_QK = RET_HEADS * RET_DK
RET_V = RET_HEADS * RET_DV
RET_CHUNK = 128
ROPE_THETA = 10000.0
LRU_WIDTH = 1280
LRU_BLOCKS = 16
LRU_BLOCK = LRU_WIDTH // LRU_BLOCKS
LRU_C = 8.0
CONV_W = 4
D_FF = -(-8 * D_MODEL // (3 * 256)) * 256
EPS = 1e-6

IN_SIZES = (RET_QK, RET_QK, RET_V, RET_V, LRU_WIDTH, LRU_WIDTH, D_MODEL, D_MODEL)
IN_TOTAL = sum(IN_SIZES)
IN_SPLITS = tuple(int(s) for s in np.cumsum(IN_SIZES)[:-1])

kernel_name = "retention_rglru_gated_hybrid_step"


def rmsnorm(x, g):
    xf = x.astype(jnp.float32)
    y = xf * lax.rsqrt(jnp.mean(xf * xf, axis=-1, keepdims=True) + EPS)
    return (y * g.astype(jnp.float32)).astype(x.dtype)


def rope(x, pos):
    half = x.shape[-1] // 2
    freqs = ROPE_THETA ** (-jnp.arange(half, dtype=jnp.float32) / half)
    ang = pos[:, None] * freqs[None, :]
    cos = jnp.cos(ang)[None, :, None, :]
    sin = jnp.sin(ang)[None, :, None, :]
    x1, x2 = x[..., :half], x[..., half:]
    return jnp.concatenate([x1 * cos - x2 * sin, x1 * sin + x2 * cos], axis=-1)


def retention(q, k, v, S0):
    B, L, H, _ = q.shape
    C = RET_CHUNK if L % RET_CHUNK == 0 else L
    N = L // C
    log_g = jnp.log(1.0 - 2.0 ** (-5.0 - jnp.arange(H, dtype=jnp.float32)))
    idx = jnp.arange(C, dtype=jnp.float32)
    diff = idx[:, None] - idx[None, :]
    decay = jnp.where(diff >= 0, jnp.exp(log_g[:, None, None] * jnp.maximum(diff, 0.0)), 0.0)
    q_dec = jnp.exp(log_g[:, None] * (idx[None, :] + 1.0))
    k_dec = jnp.exp(log_g[:, None] * (C - 1.0 - idx[None, :]))
    chunk_dec = jnp.exp(log_g * C)

    def to_chunks(t):
        return t.reshape(B, N, C, H, t.shape[-1]).transpose(1, 0, 3, 2, 4)

    def step(S, inp):
        qc, kc, vc = inp
        scores = jnp.einsum('bhid,bhjd->bhij', qc, kc) * decay[None]
        inner = jnp.einsum('bhij,bhje->bhie', scores, vc)
        cross = jnp.einsum('bhid,bhde->bhie', qc, S) * q_dec[None, :, :, None]
        S_new = S * chunk_dec[None, :, None, None] + jnp.einsum(
            'bhjd,bhje->bhde', kc * k_dec[None, :, :, None], vc)
        return S_new, inner + cross

    S, o = lax.scan(step, S0, (to_chunks(q), to_chunks(k), to_chunks(v)))
    o = o.transpose(1, 0, 3, 2, 4).reshape(B, L, H, v.shape[-1])
    return o, S


def causal_conv(x, buf, w, b):
    L = x.shape[1]
    xx = jnp.concatenate([buf.astype(x.dtype), x], axis=1)
    y = b[None, None, :]
    for j in range(CONV_W):
        y = y + xx[:, j:j + L] * w[j][None, None, :]
    return y, xx[:, -(CONV_W - 1):]


def rg_lru(xc, w_a, b_a, w_x, b_x, lam, h0):
    B, L, W = xc.shape
    xf = xc.astype(jnp.float32)
    xb = xf.reshape(B, L, LRU_BLOCKS, LRU_BLOCK)
    r = jax.nn.sigmoid(jnp.einsum('blnc,ncd->blnd', xb, w_a.astype(jnp.float32)).reshape(B, L, W) + b_a)
    i = jax.nn.sigmoid(jnp.einsum('blnc,ncd->blnd', xb, w_x.astype(jnp.float32)).reshape(B, L, W) + b_x)
    log_a = -LRU_C * r * jax.nn.softplus(-lam.astype(jnp.float32))
    a = jnp.exp(log_a)
    u = jnp.sqrt(-jnp.expm1(2.0 * log_a)) * (i * xf)
    u = u.at[:, 0].add(a[:, 0] * h0.astype(jnp.float32))

    def comb(l, rr):
        return (l[0] * rr[0], rr[0] * l[1] + rr[1])

    _, h = lax.associative_scan(comb, (a, u), axis=1)
    return h, h[:, -1]


def layer(x, pos, S_ret, h_lru, conv_buf, w_in, norm_mix, gn_gain, w_ret_out, conv_w, conv_b,
          lru_wa, lru_ba, lru_wx, lru_bx, lru_lambda, w_lru_out, b_gate, w_o, norm_ffn,
          w_gate_up, w_down):
    B, L, _ = x.shape
    h = rmsnorm(x, norm_mix)
    proj = h @ w_in
    q, k, v, g_ret, x_lru, y_lru, g_a, g_b = jnp.split(proj, IN_SPLITS, axis=-1)
    qf = rope(q.reshape(B, L, RET_HEADS, RET_DK).astype(jnp.float32), pos)
    kf = rope(k.reshape(B, L, RET_HEADS, RET_DK).astype(jnp.float32), pos) * (RET_DK ** -0.5)
    vf = v.reshape(B, L, RET_HEADS, RET_DV).astype(jnp.float32)
    o, S_new = retention(qf, kf, vf, S_ret.astype(jnp.float32))
    mu = jnp.mean(o, axis=-1, keepdims=True)
    var = jnp.mean(jnp.square(o - mu), axis=-1, keepdims=True)
    o = ((o - mu) * lax.rsqrt(var + EPS)).reshape(B, L, RET_V) * gn_gain.astype(jnp.float32)
    ret_branch = (jax.nn.silu(g_ret) * o.astype(x.dtype)) @ w_ret_out
    xc, new_buf = causal_conv(x_lru, conv_buf, conv_w, conv_b)
    hl, h_last = rg_lru(xc, lru_wa, lru_ba, lru_wx, lru_bx, lru_lambda, h_lru)
    lru_branch = (jax.nn.gelu(y_lru) * hl.astype(x.dtype)) @ w_lru_out
    merged = jax.nn.sigmoid(g_a + b_gate[0]) * ret_branch + jax.nn.sigmoid(g_b + b_gate[1]) * lru_branch
    x = x + merged @ w_o
    h2 = rmsnorm(x, norm_ffn)
    gu = h2 @ w_gate_up
    gt, up = gu[..., :D_FF], gu[..., D_FF:]
    x = x + (jax.nn.silu(gt) * up) @ w_down
    return x, S_new, h_last, new_buf


def trunk(x, pos, S_rets, h_lrus, conv_bufs, w_in, norm_mix, gn_gain, w_ret_out, conv_w, conv_b,
          lru_wa, lru_ba, lru_wx, lru_bx, lru_lambda, w_lru_out, b_gate, w_o, norm_ffn,
          w_gate_up, w_down, norm_final):
    new_S, new_h, new_c = [], [], []
    for l in range(DEPTH):
        x, S, hh, cb = layer(x, pos, S_rets[l], h_lrus[l], conv_bufs[l], w_in[l], norm_mix[l], gn_gain[l],
                             w_ret_out[l], conv_w[l], conv_b[l], lru_wa[l], lru_ba[l], lru_wx[l], lru_bx[l],
                             lru_lambda[l], w_lru_out[l], b_gate[l], w_o[l], norm_ffn[l], w_gate_up[l], w_down[l])
        new_S.append(S)
        new_h.append(hh)
        new_c.append(cb)
    return rmsnorm(x, norm_final), jnp.stack(new_S, 0), jnp.stack(new_h, 0), jnp.stack(new_c, 0)


def setup_inputs(seed: int = 0) -> dict:
    key = jax.random.key(seed)
    ks = jax.random.split(key, 24)
    f32 = jnp.float32

    def nrm(k, shape, s):
        return jax.random.normal(k, shape, f32) * s

    u = jax.random.uniform(ks[15], (DEPTH, LRU_WIDTH), f32, minval=0.9, maxval=0.999)
    a0 = u ** (1.0 / LRU_C)
    lam = jnp.log(a0) - jnp.log1p(-a0)
    return {
        "x_prompt": nrm(ks[0], (BATCH, SEQ, D_MODEL), 1.0),
        "x_sample": nrm(ks[1], (DEC_BATCH, DEC_SEQ, D_MODEL), 1.0),
        "state_ret": nrm(ks[2], (DEPTH, DEC_BATCH, RET_HEADS, RET_DK, RET_DV), 0.5),
        "state_lru": nrm(ks[3], (DEPTH, DEC_BATCH, LRU_WIDTH), 0.5),
        "state_conv": nrm(ks[4], (DEPTH, DEC_BATCH, CONV_W - 1, LRU_WIDTH), 1.0),
        "w_in": nrm(ks[5], (DEPTH, D_MODEL, IN_TOTAL), D_MODEL ** -0.5),
        "norm_mix": 1.0 + nrm(ks[6], (DEPTH, D_MODEL), 0.02),
        "gn_gain": 1.0 + nrm(ks[7], (DEPTH, RET_V), 0.02),
        "w_ret_out": nrm(ks[8], (DEPTH, RET_V, D_MODEL), RET_V ** -0.5),
        "conv_w": nrm(ks[9], (DEPTH, CONV_W, LRU_WIDTH), CONV_W ** -0.5),
        "conv_b": nrm(ks[10], (DEPTH, LRU_WIDTH), 0.02),
        "lru_wa": nrm(ks[11], (DEPTH, LRU_BLOCKS, LRU_BLOCK, LRU_BLOCK), LRU_BLOCK ** -0.5),
        "lru_ba": nrm(ks[12], (DEPTH, LRU_WIDTH), 0.02),
        "lru_wx": nrm(ks[13], (DEPTH, LRU_BLOCKS, LRU_BLOCK, LRU_BLOCK), LRU_BLOCK ** -0.5),
        "lru_bx": nrm(ks[14], (DEPTH, LRU_WIDTH), 0.02),
        "lru_lambda": lam,
        "w_lru_out": nrm(ks[16], (DEPTH, LRU_WIDTH, D_MODEL), LRU_WIDTH ** -0.5),
        "b_gate": nrm(ks[17], (DEPTH, 2, D_MODEL), 0.02),
        "w_o": nrm(ks[18], (DEPTH, D_MODEL, D_MODEL), D_MODEL ** -0.5),
        "norm_ffn": 1.0 + nrm(ks[19], (DEPTH, D_MODEL), 0.02),
        "w_gate_up": nrm(ks[20], (DEPTH, D_MODEL, 2 * D_FF), D_MODEL ** -0.5),
        "w_down": nrm(ks[21], (DEPTH, D_FF, D_MODEL), D_FF ** -0.5),
        "norm_final": 1.0 + nrm(ks[22], (D_MODEL,), 0.02),
    }


def reference(x_prompt, x_sample, state_ret, state_lru, state_conv, w_in, norm_mix, gn_gain, w_ret_out,
              conv_w, conv_b, lru_wa, lru_ba, lru_wx, lru_bx, lru_lambda, w_lru_out, b_gate, w_o,
              norm_ffn, w_gate_up, w_down, norm_final):
    Bp, Lp, _ = x_prompt.shape
    Bs, Ls, _ = x_sample.shape
    pos_p = jnp.arange(Lp, dtype=jnp.float32)
    S0 = jnp.zeros((DEPTH, Bp, RET_HEADS, RET_DK, RET_DV), jnp.float32)
    h0 = jnp.zeros((DEPTH, Bp, LRU_WIDTH), jnp.float32)
    c0 = jnp.zeros((DEPTH, Bp, CONV_W - 1, LRU_WIDTH), x_prompt.dtype)
    y_prompt, ret_p, lru_p, conv_p = trunk(
        x_prompt, pos_p, S0, h0, c0, w_in, norm_mix, gn_gain, w_ret_out, conv_w, conv_b,
        lru_wa, lru_ba, lru_wx, lru_bx, lru_lambda, w_lru_out, b_gate, w_o, norm_ffn,
        w_gate_up, w_down, norm_final)
    pos_s = PAST_LEN + jnp.arange(Ls, dtype=jnp.float32)
    y_sample, ret_s, lru_s, conv_s = trunk(
        x_sample, pos_s, state_ret, state_lru, state_conv, w_in, norm_mix, gn_gain, w_ret_out, conv_w,
        conv_b, lru_wa, lru_ba, lru_wx, lru_bx, lru_lambda, w_lru_out, b_gate, w_o, norm_ffn,
        w_gate_up, w_down, norm_final)
    return (y_prompt, y_sample, ret_p, lru_p, conv_p, ret_s, lru_s, conv_s)
```

```python
import functools

import jax
import jax.numpy as jnp
import numpy as np
from jax import lax
from jax.experimental import pallas as pl
from jax.experimental.pallas import tpu as pltpu

F32 = jnp.float32
BF16 = jnp.bfloat16

D_MODEL = 1024
PAST_LEN = 16384
RET_HEADS = 4
RET_DK = 256
RET_DV = 512
RET_QK = RET_HEADS * RET_DK
RET_V = RET_HEADS * RET_DV
RET_CHUNK = 128
ROPE_THETA = 10000.0
ROPE_HALF = RET_DK // 2
LRU_WIDTH = 1280
LRU_BLOCKS = 16
LRU_BLOCK = LRU_WIDTH // LRU_BLOCKS
LRU_C = 8.0
CONV_W = 4
D_FF = 2816
EPS = 1e-6

V7X_VMEM_BYTES = 64 * 1024 * 1024
VMEM_LIMIT_BYTES = V7X_VMEM_BYTES * 7 // 8
SUBLANES = 8
LANES = 128

LRU_TILE = int(np.lcm(LRU_BLOCK, LANES))
LRU_NTILES = LRU_WIDTH // LRU_TILE
LRU_ROWS = 256
FFN_CHUNK = 256
QKV_TILE = 512


def _params(*semantics):
    return pltpu.CompilerParams(dimension_semantics=semantics, vmem_limit_bytes=VMEM_LIMIT_BYTES)


def _rmsnorm_body(x_ref, g_ref, o_ref):
    x = x_ref[...]
    y = x * lax.rsqrt(jnp.mean(x * x, axis=-1, keepdims=True) + EPS)
    o_ref[...] = (y * g_ref[...]).astype(o_ref.dtype)


def _rmsnorm(x, gain, out_dtype, tm):
    t, d = x.shape
    return pl.pallas_call(
        _rmsnorm_body,
        grid=(t // tm,),
        in_specs=[pl.BlockSpec((tm, d), lambda i: (i, 0)), pl.BlockSpec((1, d), lambda i: (0, 0))],
        out_specs=pl.BlockSpec((tm, d), lambda i: (i, 0)),
        out_shape=jax.ShapeDtypeStruct((t, d), out_dtype),
        compiler_params=_params("parallel"),
        name="rmsnorm",
    )(x, gain.reshape(1, d))


def _qkv_body(h_ref, w_ref, cos_ref, sin_ref, o_ref):
    acc = jnp.dot(h_ref[...], w_ref[...], preferred_element_type=F32)
    j = pl.program_id(1)
    n_q = RET_QK // QKV_TILE

    @pl.when(j < 2 * n_q)
    def _():
        scale = jnp.where(j < n_q, 1.0, RET_DK ** -0.5).astype(F32)
        cos = cos_ref[...]
        sin = sin_ref[...]
        for hd in range(QKV_TILE // RET_DK):
            lo = hd * RET_DK
            x1 = acc[:, lo:lo + ROPE_HALF]
            x2 = acc[:, lo + ROPE_HALF:lo + RET_DK]
            o_ref[:, lo:lo + ROPE_HALF] = ((x1 * cos - x2 * sin) * scale).astype(o_ref.dtype)
            o_ref[:, lo + ROPE_HALF:lo + RET_DK] = ((x1 * sin + x2 * cos) * scale).astype(o_ref.dtype)

    @pl.when(j >= 2 * n_q)
    def _():
        o_ref[...] = acc.astype(o_ref.dtype)


def _qkv_proj(h, w_qkv, cos, sin, tm):
    t, d = h.shape
    n = w_qkv.shape[1]
    pos_blocks = cos.shape[0] // tm
    return pl.pallas_call(
        _qkv_body,
        grid=(t // tm, n // QKV_TILE),
        in_specs=[
            pl.BlockSpec((tm, d), lambda i, j: (i, 0)),
            pl.BlockSpec((d, QKV_TILE), lambda i, j: (0, j)),
            pl.BlockSpec((tm, ROPE_HALF), lambda i, j: (i % pos_blocks, 0)),
            pl.BlockSpec((tm, ROPE_HALF), lambda i, j: (i % pos_blocks, 0)),
        ],
        out_specs=pl.BlockSpec((tm, QKV_TILE), lambda i, j: (i, j)),
        out_shape=jax.ShapeDtypeStruct((t, n), BF16),
        compiler_params=_params("parallel", "arbitrary"),
        name="qkv_proj",
    )(h, w_qkv, cos, sin)


def _matmul_body(h_ref, w_ref, o_ref):
    o_ref[...] = jnp.dot(h_ref[...], w_ref[...], preferred_element_type=F32).astype(o_ref.dtype)


def _matmul(h, w, out_dtype, tm, tn, name):
    t, d = h.shape
    n = w.shape[1]
    return pl.pallas_call(
        _matmul_body,
        grid=(t // tm, n // tn),
        in_specs=[pl.BlockSpec((tm, d), lambda i, j: (i, 0)), pl.BlockSpec((d, tn), lambda i, j: (0, j))],
        out_specs=pl.BlockSpec((tm, tn), lambda i, j: (i, j)),
        out_shape=jax.ShapeDtypeStruct((t, n), out_dtype),
        compiler_params=_params("parallel", "arbitrary"),
        name=name,
    )(h, w)


def _group_norm_gate(o, gate, gain):
    mu = jnp.mean(o, axis=-1, keepdims=True)
    var = jnp.mean(jnp.square(o - mu), axis=-1, keepdims=True)
    on = (o - mu) * lax.rsqrt(var + EPS) * gain
    return jax.nn.silu(gate) * on


def _retention_body(q_ref, k_ref, v_ref, g_ref, gain_ref, decay_ref, qdec_ref, kdec_ref, cdec_ref,
                    o_ref, s_out_ref, s_ref):
    n_chunks = q_ref.shape[0] // RET_CHUNK
    s_ref[...] = jnp.zeros_like(s_ref)
    decay = decay_ref[0]
    qdec = qdec_ref[0]
    kdec = kdec_ref[0]
    cdec = cdec_ref[0]
    gain = gain_ref[...]

    def chunk(c, carry):
        rows = pl.ds(pl.multiple_of(c * RET_CHUNK, RET_CHUNK), RET_CHUNK)
        q = q_ref[rows, :]
        k = k_ref[rows, :]
        v = v_ref[rows, :]
        s = s_ref[...]
        scores = lax.dot_general(q, k, (((1,), (1,)), ((), ())), preferred_element_type=F32) * decay
        inner = jnp.dot(scores.astype(BF16), v, preferred_element_type=F32)
        cross = jnp.dot(q, s.astype(BF16), preferred_element_type=F32) * qdec
        kd = (k.astype(F32) * kdec).astype(BF16)
        s_ref[...] = s * cdec + lax.dot_general(kd, v, (((0,), (0,)), ((), ())), preferred_element_type=F32)
        o_ref[rows, :] = _group_norm_gate(inner + cross, g_ref[rows, :], gain).astype(o_ref.dtype)
        return carry

    lax.fori_loop(0, n_chunks, chunk, 0)
    s_out_ref[0, 0] = s_ref[...]


def _retention_tables(chunk):
    log_g = jnp.log(1.0 - 2.0 ** (-5.0 - jnp.arange(RET_HEADS, dtype=F32)))
    idx = jnp.arange(chunk, dtype=F32)
    diff = idx[:, None] - idx[None, :]
    decay = jnp.where(diff >= 0, jnp.exp(log_g[:, None, None] * jnp.maximum(diff, 0.0)), 0.0)
    q_dec = jnp.exp(log_g[:, None] * (idx[None, :] + 1.0))
    k_dec = jnp.exp(log_g[:, None] * (chunk - 1.0 - idx[None, :]))
    chunk_dec = jnp.exp(log_g * chunk)
    return decay, q_dec[:, :, None], k_dec[:, :, None], chunk_dec[:, None, None]


def _retention_prompt(qkv, g_ret, gain, batch, seq):
    t = qkv.shape[0]
    decay, q_dec, k_dec, chunk_dec = _retention_tables(RET_CHUNK)
    k_off = RET_QK // RET_DK
    v_off = 2 * RET_QK // RET_DV
    head_tab = lambda shape: pl.BlockSpec((1,) + shape, lambda b, h: (h, 0, 0))
    return pl.pallas_call(
        _retention_body,
        grid=(batch, RET_HEADS),
        in_specs=[
            pl.BlockSpec((seq, RET_DK), lambda b, h: (b, h)),
            pl.BlockSpec((seq, RET_DK), lambda b, h: (b, k_off + h)),
            pl.BlockSpec((seq, RET_DV), lambda b, h: (b, v_off + h)),
            pl.BlockSpec((seq, RET_DV), lambda b, h: (b, h)),
            pl.BlockSpec((1, RET_DV), lambda b, h: (0, h)),
            head_tab((RET_CHUNK, RET_CHUNK)),
            head_tab((RET_CHUNK, 1)),
            head_tab((RET_CHUNK, 1)),
            head_tab((1, 1)),
        ],
        out_specs=[
            pl.BlockSpec((seq, RET_DV), lambda b, h: (b, h)),
            pl.BlockSpec((1, 1, RET_DK, RET_DV), lambda b, h: (b, h, 0, 0)),
        ],
        out_shape=[
            jax.ShapeDtypeStruct((t, RET_V), BF16),
            jax.ShapeDtypeStruct((batch, RET_HEADS, RET_DK, RET_DV), F32),
        ],
        scratch_shapes=[pltpu.VMEM((RET_DK, RET_DV), F32)],
        compiler_params=_params("parallel", "arbitrary"),
        name="retention_prompt",
    )(qkv, qkv, qkv, g_ret, gain.reshape(1, RET_V), decay, q_dec, k_dec, chunk_dec)


def _retention_step_body(q_ref, k_ref, v_ref, g_ref, gain_ref, dec_ref, s_ref, o_ref, s_out_ref):
    for hd in range(RET_HEADS):
        s = s_ref[0, hd]
        qc = q_ref[0, hd * RET_DK:(hd + 1) * RET_DK, :]
        kc = k_ref[0, hd * RET_DK:(hd + 1) * RET_DK, :]
        cols = slice(hd * RET_DV, (hd + 1) * RET_DV)
        v = v_ref[0, :, cols]
        decay = dec_ref[hd:hd + 1, 0:1]
        q_dec = dec_ref[hd:hd + 1, 1:2]
        k_dec = dec_ref[hd:hd + 1, 2:3]
        chunk_dec = dec_ref[hd:hd + 1, 3:4]
        scores = jnp.sum(qc * kc, axis=0, keepdims=True) * decay
        cross = jnp.sum(qc * s, axis=0, keepdims=True) * q_dec
        s_out_ref[0, hd] = s * chunk_dec + (kc * k_dec) * v
        o = scores * v + cross
        o_ref[0, :, cols] = _group_norm_gate(o, g_ref[0, :, cols], gain_ref[:, cols])


def _retention_step(q_col, k_col, v_row, g_row, gain, state):
    batch = state.shape[0]
    decay, q_dec, k_dec, chunk_dec = _retention_tables(1)
    dec = jnp.concatenate([decay[:, :, 0], q_dec[:, :, 0], k_dec[:, :, 0], chunk_dec[:, :, 0]], axis=1)
    col = pl.BlockSpec((1, RET_QK, 1), lambda b: (b, 0, 0))
    row = pl.BlockSpec((1, 1, RET_V), lambda b: (b, 0, 0))
    st = pl.BlockSpec((1, RET_HEADS, RET_DK, RET_DV), lambda b: (b, 0, 0, 0))
    return pl.pallas_call(
        _retention_step_body,
        grid=(batch,),
        in_specs=[col, col, row, row, pl.BlockSpec((1, RET_V), lambda b: (0, 0)),
                  pl.BlockSpec((RET_HEADS, 4), lambda b: (0, 0)), st],
        out_specs=[row, st],
        out_shape=[jax.ShapeDtypeStruct((batch, 1, RET_V), F32), jax.ShapeDtypeStruct(state.shape, F32)],
        compiler_params=_params("parallel"),
        name="retention_step",
    )(q_col, k_col, v_row, g_row, gain.reshape(1, RET_V), dec, state)


def _lru_gates(xc, w_ref, ba_ref, bx_ref, lam_ref):
    g = jnp.dot(xc.astype(BF16), w_ref[0], preferred_element_type=F32)
    r = jax.nn.sigmoid(g[:, :LRU_TILE] + ba_ref[...])
    i = jax.nn.sigmoid(g[:, LRU_TILE:] + bx_ref[...])
    log_a = -LRU_C * r * jax.nn.softplus(-lam_ref[...])
    a = jnp.exp(log_a)
    th = jnp.tanh(log_a)
    u = jnp.sqrt(-2.0 * th / (1.0 - th)) * (i * xc)
    return a, u


def _lru_prompt_body(x_ref, y_ref, cw_ref, cb_ref, w_ref, ba_ref, bx_ref, lam_ref,
                     o_ref, hl_ref, xpad_ref, a_ref, u_ref):
    seq = x_ref.shape[0]
    tw = x_ref.shape[1]
    pad = SUBLANES
    xpad_ref[0:pad, :] = jnp.zeros((pad, tw), F32)
    row = lax.broadcasted_iota(jnp.int32, (SUBLANES, tw), 0)
    carry = jnp.zeros((SUBLANES, tw), F32)

    def group(g, h_prev):
        rows = pl.ds(pl.multiple_of(g * SUBLANES, SUBLANES), SUBLANES)
        a = a_ref[rows, :]
        u = u_ref[rows, :]
        for sh in (1, 2, 4):
            keep = row >= sh
            u = jnp.where(keep, a * pltpu.roll(u, sh, 0) + u, u)
            a = jnp.where(keep, a * pltpu.roll(a, sh, 0), a)
        h = u + a * h_prev
        u_ref[rows, :] = h
        return jnp.broadcast_to(h[SUBLANES - 1:SUBLANES, :], (SUBLANES, tw))

    for r0 in range(0, seq, LRU_ROWS):
        xpad_ref[pad + r0:pad + r0 + LRU_ROWS, :] = x_ref[r0:r0 + LRU_ROWS, :]
        xc = cb_ref[...]
        for j in range(CONV_W):
            lo = pad + r0 - (CONV_W - 1) + j
            xc = xc + xpad_ref[lo:lo + LRU_ROWS, :] * cw_ref[j:j + 1, :]
        a, u = _lru_gates(xc, w_ref, ba_ref, bx_ref, lam_ref)
        a_ref[...] = a
        u_ref[...] = u
        carry = lax.fori_loop(0, LRU_ROWS // SUBLANES, group, carry)
        o_ref[r0:r0 + LRU_ROWS, :] = (jax.nn.gelu(y_ref[r0:r0 + LRU_ROWS, :]) * u_ref[...]).astype(o_ref.dtype)
    hl_ref[0] = carry[0:1, :]


def _lru_specs(rows, index):
    return pl.BlockSpec((rows, LRU_TILE), index)


def _lru_prompt(xy, conv_w, conv_b, w_bd, b_a, b_x, lam, batch, seq):
    t = xy.shape[0]
    vec = lambda rows: _lru_specs(rows, lambda b, w: (0, w))
    return pl.pallas_call(
        _lru_prompt_body,
        grid=(batch, LRU_NTILES),
        in_specs=[
            _lru_specs(seq, lambda b, w: (b, w)),
            _lru_specs(seq, lambda b, w: (b, LRU_NTILES + w)),
            vec(CONV_W), vec(1),
            pl.BlockSpec((1, LRU_TILE, 2 * LRU_TILE), lambda b, w: (w, 0, 0)),
            vec(1), vec(1), vec(1),
        ],
        out_specs=[
            _lru_specs(seq, lambda b, w: (b, w)),
            pl.BlockSpec((1, 1, LRU_TILE), lambda b, w: (b, 0, w)),
        ],
        out_shape=[
            jax.ShapeDtypeStruct((t, LRU_WIDTH), BF16),
            jax.ShapeDtypeStruct((batch, 1, LRU_WIDTH), F32),
        ],
        scratch_shapes=[
            pltpu.VMEM((seq + SUBLANES, LRU_TILE), F32),
            pltpu.VMEM((LRU_ROWS, LRU_TILE), F32),
            pltpu.VMEM((LRU_ROWS, LRU_TILE), F32),
        ],
        compiler_params=_params("parallel", "arbitrary"),
        name="lru_prompt",
    )(xy, xy, conv_w, conv_b.reshape(1, -1), w_bd, b_a.reshape(1, -1), b_x.reshape(1, -1), lam.reshape(1, -1))


def _lru_step_body(x_ref, y_ref, b0_ref, b1_ref, b2_ref, h0_ref, cw_ref, cb_ref, w_ref, ba_ref, bx_ref, lam_ref,
                   o_ref, h_ref):
    taps = (b0_ref, b1_ref, b2_ref, x_ref)
    xc = cb_ref[...]
    for j in range(CONV_W):
        xc = xc + taps[j][...] * cw_ref[j:j + 1, :]
    a, u = _lru_gates(xc, w_ref, ba_ref, bx_ref, lam_ref)
    h = u + a * h0_ref[...]
    h_ref[...] = h
    o_ref[...] = (jax.nn.gelu(y_ref[...]) * h).astype(o_ref.dtype)


def _lru_step(xy, conv_buf, h0, conv_w, conv_b, w_bd, b_a, b_x, lam):
    batch = xy.shape[0]
    tile = lambda off: _lru_specs(batch, lambda w: (0, off + w))
    vec = lambda rows: _lru_specs(rows, lambda w: (0, w))
    return pl.pallas_call(
        _lru_step_body,
        grid=(LRU_NTILES,),
        in_specs=[
            tile(0), tile(LRU_NTILES),
            tile(0), tile(LRU_NTILES), tile(2 * LRU_NTILES),
            tile(0),
            vec(CONV_W), vec(1),
            pl.BlockSpec((1, LRU_TILE, 2 * LRU_TILE), lambda w: (w, 0, 0)),
            vec(1), vec(1), vec(1),
        ],
        out_specs=[tile(0), tile(0)],
        out_shape=[jax.ShapeDtypeStruct((batch, LRU_WIDTH), BF16), jax.ShapeDtypeStruct((batch, LRU_WIDTH), F32)],
        compiler_params=_params("parallel"),
        name="lru_step",
    )(xy, xy, conv_buf, conv_buf, conv_buf, h0, conv_w, conv_b.reshape(1, -1), w_bd,
      b_a.reshape(1, -1), b_x.reshape(1, -1), lam.reshape(1, -1))


def _merge_body(ret_ref, lru_ref, gab_ref, x_ref, wr_ref, wl_ref, wo_ref, bg_ref, o_ref):
    ret_branch = jnp.dot(ret_ref[...], wr_ref[...], preferred_element_type=F32)
    lru_branch = jnp.dot(lru_ref[...], wl_ref[...], preferred_element_type=F32)
    gate_a = jax.nn.sigmoid(gab_ref[:, :D_MODEL] + bg_ref[0:1, :])
    gate_b = jax.nn.sigmoid(gab_ref[:, D_MODEL:] + bg_ref[1:2, :])
    merged = gate_a * ret_branch + gate_b * lru_branch
    o_ref[...] = x_ref[...] + jnp.dot(merged.astype(BF16), wo_ref[...], preferred_element_type=F32)


def _merge(ret_g, lru_g, gab, x, w_ret, w_lru, w_o, b_gate, tm):
    t = x.shape[0]
    rows = lambda n: pl.BlockSpec((tm, n), lambda i: (i, 0))
    full = lambda a: pl.BlockSpec(a.shape, lambda i: (0, 0))
    return pl.pallas_call(
        _merge_body,
        grid=(t // tm,),
        in_specs=[rows(RET_V), rows(LRU_WIDTH), rows(2 * D_MODEL), rows(D_MODEL),
                  full(w_ret), full(w_lru), full(w_o), full(b_gate)],
        out_specs=rows(D_MODEL),
        out_shape=jax.ShapeDtypeStruct((t, D_MODEL), F32),
        compiler_params=_params("parallel"),
        name="merge_out",
    )(ret_g, lru_g, gab, x, w_ret, w_lru, w_o, b_gate)


def _ffn_body(x_ref, g_ref, wgu_ref, wd_ref, o_ref):
    x = x_ref[...]
    h = (x * lax.rsqrt(jnp.mean(x * x, axis=-1, keepdims=True) + EPS) * g_ref[...]).astype(BF16)
    acc = jnp.zeros(x.shape, F32)
    for c in range(0, D_FF, FFN_CHUNK):
        gate = jnp.dot(h, wgu_ref[:, c:c + FFN_CHUNK], preferred_element_type=F32)
        up = jnp.dot(h, wgu_ref[:, D_FF + c:D_FF + c + FFN_CHUNK], preferred_element_type=F32)
        act = (jax.nn.silu(gate) * up).astype(BF16)
        acc = acc + jnp.dot(act, wd_ref[c:c + FFN_CHUNK, :], preferred_element_type=F32)
    o_ref[...] = x + acc


def _ffn(x, gain, w_gate_up, w_down, tm):
    t = x.shape[0]
    rows = pl.BlockSpec((tm, D_MODEL), lambda i: (i, 0))
    full = lambda a: pl.BlockSpec(a.shape, lambda i: (0, 0))
    return pl.pallas_call(
        _ffn_body,
        grid=(t // tm,),
        in_specs=[rows, pl.BlockSpec((1, D_MODEL), lambda i: (0, 0)), full(w_gate_up), full(w_down)],
        out_specs=rows,
        out_shape=jax.ShapeDtypeStruct((t, D_MODEL), F32),
        compiler_params=_params("parallel"),
        name="swiglu",
    )(x, gain.reshape(1, D_MODEL), w_gate_up, w_down)


def _rope_tables(pos):
    freqs = ROPE_THETA ** (-jnp.arange(ROPE_HALF, dtype=F32) / ROPE_HALF)
    ang = pos[:, None] * freqs[None, :]
    return jnp.cos(ang), jnp.sin(ang)


def _block_diag_gates(w_a, w_x):
    per_tile = LRU_TILE // LRU_BLOCK
    eye = jnp.eye(per_tile, dtype=w_a.dtype)

    def dense(w):
        w = w.reshape(LRU_NTILES, per_tile, LRU_BLOCK, LRU_BLOCK)
        return jnp.einsum("tncd,nm->tncmd", w, eye).reshape(LRU_NTILES, LRU_TILE, LRU_TILE)

    return jnp.concatenate([dense(w_a), dense(w_x)], axis=-1).astype(BF16)


def _layer_weights(l, w_in, norm_mix, gn_gain, w_ret_out, conv_w, conv_b, lru_wa, lru_ba, lru_wx, lru_bx,
                   lru_lambda, w_lru_out, b_gate, w_o, norm_ffn, w_gate_up, w_down):
    w = w_in[l].astype(BF16)
    qkv_end = 2 * RET_QK + RET_V
    gret_end = qkv_end + RET_V
    xy_end = gret_end + 2 * LRU_WIDTH
    return dict(
        norm_mix=norm_mix[l], w_qkv=w[:, :qkv_end], w_gret=w[:, qkv_end:gret_end], w_xy=w[:, gret_end:xy_end],
        w_gab=w[:, xy_end:], gn_gain=gn_gain[l], w_ret_out=w_ret_out[l].astype(BF16),
        conv_w=conv_w[l], conv_b=conv_b[l], w_bd=_block_diag_gates(lru_wa[l], lru_wx[l]),
        lru_ba=lru_ba[l], lru_bx=lru_bx[l], lru_lambda=lru_lambda[l],
        w_lru_out=w_lru_out[l].astype(BF16), b_gate=b_gate[l], w_o=w_o[l].astype(BF16),
        norm_ffn=norm_ffn[l], w_gate_up=w_gate_up[l].astype(BF16), w_down=w_down[l].astype(BF16))


def _token_parallel_front(x, p, cos, sin, tm):
    h = _rmsnorm(x, p["norm_mix"], BF16, tm)
    qkv = _qkv_proj(h, p["w_qkv"], cos, sin, tm)
    g_ret = _matmul(h, p["w_gret"], F32, tm, 512, "gret_proj")
    xy = _matmul(h, p["w_xy"], F32, tm, LRU_TILE, "xy_proj")
    gab = _matmul(h, p["w_gab"], F32, tm, 512, "gab_proj")
    return qkv, g_ret, xy, gab


def _token_parallel_back(x, ret_g, lru_g, gab, p, tm):
    x = _merge(ret_g, lru_g, gab, x, p["w_ret_out"], p["w_lru_out"], p["w_o"], p["b_gate"], tm)
    return _ffn(x, p["norm_ffn"], p["w_gate_up"], p["w_down"], tm)


def _prompt_layer(x, p, cos, sin, batch, seq):
    tm = 1024
    qkv, g_ret, xy, gab = _token_parallel_front(x, p, cos, sin, tm)
    ret_g, s_new = _retention_prompt(qkv, g_ret, p["gn_gain"], batch, seq)
    lru_g, h_last = _lru_prompt(xy, p["conv_w"], p["conv_b"], p["w_bd"], p["lru_ba"], p["lru_bx"],
                                p["lru_lambda"], batch, seq)
    new_buf = xy.reshape(batch, seq, 2 * LRU_WIDTH)[:, seq - (CONV_W - 1):, :LRU_WIDTH]
    x = _token_parallel_back(x, ret_g, lru_g, gab, p, 512)
    return x, s_new, h_last.reshape(batch, LRU_WIDTH), new_buf


def _sample_layer(x, p, cos, sin, s_ret, h_lru, conv_buf):
    batch = x.shape[0]
    tm = batch
    qkv, g_ret, xy, gab = _token_parallel_front(x, p, cos, sin, tm)
    q_col = qkv[:, :RET_QK].astype(F32).reshape(batch, RET_QK, 1)
    k_col = qkv[:, RET_QK:2 * RET_QK].astype(F32).reshape(batch, RET_QK, 1)
    v_row = qkv[:, 2 * RET_QK:].astype(F32).reshape(batch, 1, RET_V)
    ret_g, s_new = _retention_step(q_col, k_col, v_row, g_ret.reshape(batch, 1, RET_V), p["gn_gain"], s_ret)
    ret_g = ret_g.reshape(batch, RET_V).astype(BF16)
    buf2d = conv_buf.reshape(batch, (CONV_W - 1) * LRU_WIDTH)
    lru_g, h_new = _lru_step(xy, buf2d, h_lru, p["conv_w"], p["conv_b"], p["w_bd"], p["lru_ba"], p["lru_bx"],
                             p["lru_lambda"])
    new_buf = jnp.concatenate([conv_buf[:, 1:], xy[:, None, :LRU_WIDTH]], axis=1)
    x = _token_parallel_back(x, ret_g, lru_g, gab, p, tm)
    return x, s_new, h_new, new_buf


def kernel(x_prompt, x_sample, state_ret, state_lru, state_conv, w_in, norm_mix, gn_gain, w_ret_out, conv_w,
           conv_b, lru_wa, lru_ba, lru_wx, lru_bx, lru_lambda, w_lru_out, b_gate, w_o, norm_ffn, w_gate_up,
           w_down, norm_final):
    bp, lp, _ = x_prompt.shape
    bs, ls, _ = x_sample.shape
    assert ls == 1 and lp % RET_CHUNK == 0 and lp % LRU_ROWS == 0
    depth = w_in.shape[0]
    layers = [_layer_weights(l, w_in, norm_mix, gn_gain, w_ret_out, conv_w, conv_b, lru_wa, lru_ba, lru_wx, lru_bx,
                             lru_lambda, w_lru_out, b_gate, w_o, norm_ffn, w_gate_up, w_down)
              for l in range(depth)]

    cos_p, sin_p = _rope_tables(jnp.arange(lp, dtype=F32))
    cos_s, sin_s = _rope_tables(PAST_LEN + jnp.arange(ls, dtype=F32))
    cos_s = jnp.broadcast_to(cos_s, (bs, ROPE_HALF))
    sin_s = jnp.broadcast_to(sin_s, (bs, ROPE_HALF))

    xp = x_prompt.reshape(bp * lp, D_MODEL)
    xs = x_sample.reshape(bs * ls, D_MODEL)
    ret_p, lru_p, conv_p, ret_s, lru_s, conv_s = [], [], [], [], [], []
    for l, p in enumerate(layers):
        xp, s, h, c = _prompt_layer(xp, p, cos_p, sin_p, bp, lp)
        ret_p.append(s), lru_p.append(h), conv_p.append(c)
        xs, s, h, c = _sample_layer(xs, p, cos_s, sin_s, state_ret[l], state_lru[l], state_conv[l])
        ret_s.append(s), lru_s.append(h), conv_s.append(c)

    y_prompt = _rmsnorm(xp, norm_final, F32, 1024).reshape(bp, lp, D_MODEL)
    y_sample = _rmsnorm(xs, norm_final, F32, bs).reshape(bs, ls, D_MODEL)
    stack = lambda parts: jnp.stack(parts, 0)
    return (y_prompt, y_sample, stack(ret_p), stack(lru_p), stack(conv_p),
            stack(ret_s), stack(lru_s), stack(conv_s))
```

```python
import jax
import jax.numpy as jnp
import numpy as np
from jax import lax
from jax.experimental import pallas as pl
from jax.experimental.pallas import tpu as pltpu

F32 = jnp.float32
BF16 = jnp.bfloat16

D_MODEL = 1024
PAST_LEN = 16384
RET_HEADS = 4
RET_DK = 256
RET_DV = 512
RET_QK = RET_HEADS * RET_DK
RET_V = RET_HEADS * RET_DV
RET_CHUNK = 128
ROPE_THETA = 10000.0
ROPE_HALF = RET_DK // 2
LRU_WIDTH = 1280
LRU_BLOCKS = 16
LRU_BLOCK = LRU_WIDTH // LRU_BLOCKS
LRU_C = 8.0
CONV_W = 4
D_FF = 2816
EPS = 1e-6

V7X_VMEM_BYTES = 64 * 1024 * 1024
VMEM_LIMIT_BYTES = V7X_VMEM_BYTES * 7 // 8
SUBLANES = 8
LANES = 128

LRU_TILE = int(np.lcm(LRU_BLOCK, LANES))
LRU_NTILES = LRU_WIDTH // LRU_TILE
LRU_ROWS = 256
LRU_STEPS = 256
RET_ROWS = 512
STEP_SEQS = 4
FFN_CHUNK = 256
QKV_TILE = 512


def _params(*semantics):
    return pltpu.CompilerParams(dimension_semantics=semantics, vmem_limit_bytes=VMEM_LIMIT_BYTES)


def _rmsnorm_body(x_ref, g_ref, o_ref):
    x = x_ref[...]
    y = x * lax.rsqrt(jnp.mean(x * x, axis=-1, keepdims=True) + EPS)
    o_ref[...] = (y * g_ref[...]).astype(o_ref.dtype)


def _rmsnorm(x, gain, out_dtype, tm):
    t, d = x.shape
    return pl.pallas_call(
        _rmsnorm_body,
        grid=(t // tm,),
        in_specs=[pl.BlockSpec((tm, d), lambda i: (i, 0)), pl.BlockSpec((1, d), lambda i: (0, 0))],
        out_specs=pl.BlockSpec((tm, d), lambda i: (i, 0)),
        out_shape=jax.ShapeDtypeStruct((t, d), out_dtype),
        compiler_params=_params("parallel"),
        name="rmsnorm",
    )(x, gain.reshape(1, d))


def _rmsnorm_two_orders_body(x_ref, g_ref, o_ref, ot_ref):
    x = x_ref[...]
    y = (x * lax.rsqrt(jnp.mean(x * x, axis=-1, keepdims=True) + EPS) * g_ref[...]).astype(o_ref.dtype)
    o_ref[...] = y
    ot_ref[...] = y


def _rmsnorm_two_orders(x, gain, batch, seq, tm):
    d = x.shape[1]
    nblk = seq // tm
    return pl.pallas_call(
        _rmsnorm_two_orders_body,
        grid=(batch, nblk),
        in_specs=[pl.BlockSpec((tm, d), lambda b, i: (b * nblk + i, 0)), pl.BlockSpec((1, d), lambda b, i: (0, 0))],
        out_specs=[pl.BlockSpec((tm, d), lambda b, i: (b * nblk + i, 0)), pl.BlockSpec((tm, d), lambda b, i: (i, b))],
        out_shape=[jax.ShapeDtypeStruct((batch * seq, d), BF16), jax.ShapeDtypeStruct((seq, batch * d), BF16)],
        compiler_params=_params("parallel", "parallel"),
        name="rmsnorm_two_orders",
    )(x, gain.reshape(1, d))


def _qkv_body(h_ref, w_ref, cos_ref, sin_ref, o_ref):
    acc = jnp.dot(h_ref[...], w_ref[...], preferred_element_type=F32)
    j = pl.program_id(1)
    n_q = RET_QK // QKV_TILE

    @pl.when(j < 2 * n_q)
    def _():
        scale = jnp.where(j < n_q, 1.0, RET_DK ** -0.5).astype(F32)
        cos = cos_ref[...]
        sin = sin_ref[...]
        for hd in range(QKV_TILE // RET_DK):
            lo = hd * RET_DK
            x1 = acc[:, lo:lo + ROPE_HALF]
            x2 = acc[:, lo + ROPE_HALF:lo + RET_DK]
            o_ref[:, lo:lo + ROPE_HALF] = ((x1 * cos - x2 * sin) * scale).astype(o_ref.dtype)
            o_ref[:, lo + ROPE_HALF:lo + RET_DK] = ((x1 * sin + x2 * cos) * scale).astype(o_ref.dtype)

    @pl.when(j >= 2 * n_q)
    def _():
        o_ref[...] = acc.astype(o_ref.dtype)


def _qkv_proj(h, w_qkv, cos, sin, tm):
    t, d = h.shape
    n = w_qkv.shape[1]
    pos_blocks = cos.shape[0] // tm
    return pl.pallas_call(
        _qkv_body,
        grid=(t // tm, n // QKV_TILE),
        in_specs=[
            pl.BlockSpec((tm, d), lambda i, j: (i, 0)),
            pl.BlockSpec((d, QKV_TILE), lambda i, j: (0, j)),
            pl.BlockSpec((tm, ROPE_HALF), lambda i, j: (i % pos_blocks, 0)),
            pl.BlockSpec((tm, ROPE_HALF), lambda i, j: (i % pos_blocks, 0)),
        ],
        out_specs=pl.BlockSpec((tm, QKV_TILE), lambda i, j: (i, j)),
        out_shape=jax.ShapeDtypeStruct((t, n), BF16),
        compiler_params=_params("parallel", "arbitrary"),
        name="qkv_proj",
    )(h, w_qkv, cos, sin)


def _matmul_body(h_ref, w_ref, o_ref):
    o_ref[...] = jnp.dot(h_ref[...], w_ref[...], preferred_element_type=F32).astype(o_ref.dtype)


def _matmul(h, w, out_dtype, tm, tn, name):
    t, d = h.shape
    n = w.shape[1]
    return pl.pallas_call(
        _matmul_body,
        grid=(t // tm, n // tn),
        in_specs=[pl.BlockSpec((tm, d), lambda i, j: (i, 0)), pl.BlockSpec((d, tn), lambda i, j: (0, j))],
        out_specs=pl.BlockSpec((tm, tn), lambda i, j: (i, j)),
        out_shape=jax.ShapeDtypeStruct((t, n), out_dtype),
        compiler_params=_params("parallel", "arbitrary"),
        name=name,
    )(h, w)


def _group_norm_gate(o, gate, gain):
    mu = jnp.mean(o, axis=-1, keepdims=True)
    var = jnp.mean(jnp.square(o - mu), axis=-1, keepdims=True)
    on = (o - mu) * lax.rsqrt(var + EPS) * gain
    return jax.nn.silu(gate) * on


def _retention_body(q_ref, k_ref, v_ref, g_ref, gain_ref, decay_ref, qdec_ref, kdec_ref, cdec_ref,
                    o_ref, s_out_ref, s_ref):
    n_chunks = q_ref.shape[0] // RET_CHUNK
    blk = pl.program_id(1)

    @pl.when(blk == 0)
    def _():
        s_ref[...] = jnp.zeros_like(s_ref)

    def chunk(c, carry):
        rows = pl.ds(pl.multiple_of(c * RET_CHUNK, RET_CHUNK), RET_CHUNK)
        for hd in range(RET_HEADS):
            qk_cols = slice(hd * RET_DK, (hd + 1) * RET_DK)
            v_cols = slice(hd * RET_DV, (hd + 1) * RET_DV)
            q = q_ref[rows, qk_cols]
            k = k_ref[rows, qk_cols]
            v = v_ref[rows, v_cols]
            s = s_ref[hd]
            scores = lax.dot_general(q, k, (((1,), (1,)), ((), ())), preferred_element_type=F32) * decay_ref[hd]
            inner = jnp.dot(scores.astype(BF16), v, preferred_element_type=F32)
            cross = jnp.dot(q, s.astype(BF16), preferred_element_type=F32) * qdec_ref[hd]
            kd = (k.astype(F32) * kdec_ref[hd]).astype(BF16)
            s_ref[hd] = s * cdec_ref[hd] + lax.dot_general(kd, v, (((0,), (0,)), ((), ())),
                                                            preferred_element_type=F32)
            o_ref[rows, v_cols] = _group_norm_gate(inner + cross, g_ref[rows, v_cols],
                                                   gain_ref[:, v_cols]).astype(o_ref.dtype)
        return carry

    lax.fori_loop(0, n_chunks, chunk, 0)

    @pl.when(blk == pl.num_programs(1) - 1)
    def _():
        s_out_ref[0] = s_ref[...]


def _retention_tables(chunk):
    log_g = jnp.log(1.0 - 2.0 ** (-5.0 - jnp.arange(RET_HEADS, dtype=F32)))
    idx = jnp.arange(chunk, dtype=F32)
    diff = idx[:, None] - idx[None, :]
    decay = jnp.where(diff >= 0, jnp.exp(log_g[:, None, None] * jnp.maximum(diff, 0.0)), 0.0)
    q_dec = jnp.exp(log_g[:, None] * (idx[None, :] + 1.0))
    k_dec = jnp.exp(log_g[:, None] * (chunk - 1.0 - idx[None, :]))
    chunk_dec = jnp.exp(log_g * chunk)
    return decay, q_dec[:, :, None], k_dec[:, :, None], chunk_dec[:, None, None]


def _retention_prompt(qkv, g_ret, gain, batch, seq):
    t = qkv.shape[0]
    decay, q_dec, k_dec, chunk_dec = _retention_tables(RET_CHUNK)
    nblk = seq // RET_ROWS
    full = lambda a: pl.BlockSpec(a.shape, lambda b, j: (0,) * a.ndim)
    state_shape = (batch, RET_HEADS, RET_DK, RET_DV)
    return pl.pallas_call(
        _retention_body,
        grid=(batch, nblk),
        in_specs=[
            pl.BlockSpec((RET_ROWS, RET_QK), lambda b, j: (b * nblk + j, 0)),
            pl.BlockSpec((RET_ROWS, RET_QK), lambda b, j: (b * nblk + j, 1)),
            pl.BlockSpec((RET_ROWS, RET_V), lambda b, j: (b * nblk + j, 2 * RET_QK // RET_V)),
            pl.BlockSpec((RET_ROWS, RET_V), lambda b, j: (b * nblk + j, 0)),
            pl.BlockSpec((1, RET_V), lambda b, j: (0, 0)),
            full(decay), full(q_dec), full(k_dec), full(chunk_dec),
        ],
        out_specs=[
            pl.BlockSpec((RET_ROWS, RET_V), lambda b, j: (b * nblk + j, 0)),
            pl.BlockSpec((1,) + state_shape[1:], lambda b, j: (b, 0, 0, 0)),
        ],
        out_shape=[jax.ShapeDtypeStruct((t, RET_V), BF16), jax.ShapeDtypeStruct(state_shape, F32)],
        scratch_shapes=[pltpu.VMEM(state_shape[1:], F32)],
        compiler_params=_params("parallel", "arbitrary"),
        name="retention_prompt",
    )(qkv, qkv, qkv, g_ret, gain.reshape(1, RET_V), decay, q_dec, k_dec, chunk_dec)


def _retention_step_body(qt_ref, kt_ref, q_ref, k_ref, v_ref, g_ref, gain_ref, dec_ref, s_ref, *rest):
    o_ref, s_out_ref = rest[-2:]
    n_seq = qt_ref.shape[1]
    seq_ids = lax.broadcasted_iota(jnp.int32, (n_seq, RET_DV), 0)
    for n in range(STEP_SEQS):
        one_hot = (seq_ids == pl.program_id(0) * STEP_SEQS + n).astype(BF16)
        for hd in range(RET_HEADS):
            qk_cols = slice(hd * RET_DK, (hd + 1) * RET_DK)
            cols = slice(hd * RET_DV, (hd + 1) * RET_DV)
            q_spread = jnp.dot(qt_ref[qk_cols, :], one_hot, preferred_element_type=F32)
            k_spread = jnp.dot(kt_ref[qk_cols, :], one_hot, preferred_element_type=F32)
            s = s_ref[0, n, hd]
            v = v_ref[n, :, cols]
            decay = dec_ref[hd:hd + 1, 0:1]
            q_dec = dec_ref[hd:hd + 1, 1:2]
            k_dec = dec_ref[hd:hd + 1, 2:3]
            chunk_dec = dec_ref[hd:hd + 1, 3:4]
            scores = jnp.sum(q_ref[n, :, qk_cols] * k_ref[n, :, qk_cols], axis=1, keepdims=True) * decay
            cross = jnp.sum(q_spread * s, axis=0, keepdims=True) * q_dec
            s_out_ref[0, n, hd] = s * chunk_dec + k_spread * (k_dec * v)
            o = scores * v + cross
            o_ref[n, :, cols] = _group_norm_gate(o, g_ref[n, :, cols], gain_ref[:, cols])


def _retention_step(qkv, g_ret, gain, states, layer, prev_states):
    batch = qkv.shape[0]
    decay, q_dec, k_dec, chunk_dec = _retention_tables(1)
    dec = jnp.concatenate([decay[:, :, 0], q_dec[:, :, 0], k_dec[:, :, 0], chunk_dec[:, :, 0]], axis=1)
    q = qkv[:, :RET_QK]
    k = qkv[:, RET_QK:2 * RET_QK]
    rows = lambda a: a.astype(F32).reshape(batch, 1, a.shape[1])
    row = lambda n: pl.BlockSpec((STEP_SEQS, 1, n), lambda i: (i, 0, 0))
    full = lambda a: pl.BlockSpec(a.shape, lambda i: (0, 0))
    st = pl.BlockSpec((1, STEP_SEQS) + states.shape[2:], lambda i: (layer, i, 0, 0, 0))
    operands = [q.T, k.T, rows(q), rows(k), rows(qkv[:, 2 * RET_QK:]), rows(g_ret), gain.reshape(1, RET_V), dec,
                states]
    in_specs = [full(operands[0]), full(operands[1]), row(RET_QK), row(RET_QK), row(RET_V), row(RET_V),
                full(operands[6]), full(dec), st]
    aliases = {}
    if prev_states is not None:
        aliases = {len(operands): 1}
        operands.append(prev_states)
        in_specs.append(pl.BlockSpec(memory_space=pl.ANY))
    return pl.pallas_call(
        _retention_step_body,
        grid=(batch // STEP_SEQS,),
        in_specs=in_specs,
        out_specs=[row(RET_V), st],
        out_shape=[jax.ShapeDtypeStruct((batch, 1, RET_V), F32), jax.ShapeDtypeStruct(states.shape, F32)],
        input_output_aliases=aliases,
        compiler_params=_params("parallel"),
        name="retention_step",
    )(*operands)


def _lru_gates(xc, w_ref, ba_ref, bx_ref, lam_ref):
    g = jnp.dot(xc.astype(BF16), w_ref[0], preferred_element_type=F32)
    r = jax.nn.sigmoid(g[:, :LRU_TILE] + ba_ref[...])
    i = jax.nn.sigmoid(g[:, LRU_TILE:] + bx_ref[...])
    log_a = -LRU_C * r * jax.nn.softplus(-lam_ref[...])
    a = jnp.exp(log_a)
    th = jnp.tanh(log_a)
    u = jnp.sqrt(-2.0 * th / (1.0 - th)) * (i * xc)
    return a, u


def _lru_prompt_body(h_ref, wxy_ref, cw_ref, cb_ref, w_ref, ba_ref, bx_ref, lam_ref,
                     o_ref, hl_ref, xt_ref, xpad_ref, state_ref):
    rows = h_ref.shape[0]
    tw = o_ref.shape[1]
    tail = (CONV_W - 1) * SUBLANES
    blk = pl.program_id(1)

    @pl.when(blk == 0)
    def _():
        xpad_ref[0:tail, :] = jnp.zeros((tail, tw), F32)
        state_ref[...] = jnp.zeros_like(state_ref)

    h = state_ref[...]
    for r0 in range(0, rows, LRU_ROWS):
        xy = jnp.dot(h_ref[r0:r0 + LRU_ROWS, :], wxy_ref[0], preferred_element_type=F32)
        xpad_ref[tail:tail + LRU_ROWS, :] = xy[:, :tw]
        xc = cb_ref[...]
        for j in range(CONV_W):
            xc = xc + xpad_ref[j * SUBLANES:j * SUBLANES + LRU_ROWS, :] * cw_ref[j:j + 1, :]
        a, u = _lru_gates(xc, w_ref, ba_ref, bx_ref, lam_ref)
        steps = []
        for g in range(0, LRU_ROWS, SUBLANES):
            h = a[g:g + SUBLANES, :] * h + u[g:g + SUBLANES, :]
            steps.append(h)
        o_ref[r0:r0 + LRU_ROWS, :] = (jax.nn.gelu(xy[:, tw:]) * jnp.concatenate(steps, axis=0)).astype(o_ref.dtype)
        xpad_ref[0:tail, :] = xpad_ref[LRU_ROWS:LRU_ROWS + tail, :]
    state_ref[...] = h

    @pl.when(blk == pl.num_programs(1) - 1)
    def _():
        hl_ref[...] = h
        xt_ref[...] = xpad_ref[0:tail, :]


def _lru_specs(rows, index):
    return pl.BlockSpec((rows, LRU_TILE), index)


def _lru_prompt(h_t, w_xy, conv_w, conv_b, w_bd, b_a, b_x, lam, batch, seq):
    assert batch == SUBLANES
    rows = LRU_STEPS * batch
    tail = (CONV_W - 1) * batch
    vec = lambda r: _lru_specs(r, lambda w, t: (0, w))
    mat = lambda a: pl.BlockSpec((1,) + a.shape[1:], lambda w, t: (w, 0, 0))
    return pl.pallas_call(
        _lru_prompt_body,
        grid=(LRU_NTILES, seq // LRU_STEPS),
        in_specs=[
            pl.BlockSpec((rows, D_MODEL), lambda w, t: (t, 0)),
            mat(w_xy), vec(CONV_W), vec(1), mat(w_bd), vec(1), vec(1), vec(1),
        ],
        out_specs=[_lru_specs(rows, lambda w, t: (t, w)), vec(batch), vec(tail)],
        out_shape=[
            jax.ShapeDtypeStruct((seq * batch, LRU_WIDTH), BF16),
            jax.ShapeDtypeStruct((batch, LRU_WIDTH), F32),
            jax.ShapeDtypeStruct((tail, LRU_WIDTH), F32),
        ],
        scratch_shapes=[pltpu.VMEM((tail + LRU_ROWS, LRU_TILE), F32), pltpu.VMEM((batch, LRU_TILE), F32)],
        compiler_params=_params("parallel", "arbitrary"),
        name="lru_prompt",
    )(h_t, w_xy, conv_w, conv_b.reshape(1, -1), w_bd, b_a.reshape(1, -1), b_x.reshape(1, -1), lam.reshape(1, -1))


def _lru_step_body(h_ref, wxy_ref, b0_ref, b1_ref, b2_ref, h0_ref, cw_ref, cb_ref, w_ref, ba_ref, bx_ref, lam_ref,
                   o_ref, hn_ref, x_ref):
    tw = o_ref.shape[1]
    xy = jnp.dot(h_ref[...], wxy_ref[0], preferred_element_type=F32)
    x = xy[:, :tw]
    x_ref[...] = x
    taps = (b0_ref[...], b1_ref[...], b2_ref[...], x)
    xc = cb_ref[...]
    for j in range(CONV_W):
        xc = xc + taps[j] * cw_ref[j:j + 1, :]
    a, u = _lru_gates(xc, w_ref, ba_ref, bx_ref, lam_ref)
    h = u + a * h0_ref[...]
    hn_ref[...] = h
    o_ref[...] = (jax.nn.gelu(xy[:, tw:]) * h).astype(o_ref.dtype)


def _lru_step(h, w_xy, conv_buf, h0, conv_w, conv_b, w_bd, b_a, b_x, lam):
    batch = h.shape[0]
    tile = lambda off: _lru_specs(batch, lambda w: (0, off + w))
    vec = lambda rows: _lru_specs(rows, lambda w: (0, w))
    mat = lambda a: pl.BlockSpec((1,) + a.shape[1:], lambda w: (w, 0, 0))
    f32_out = jax.ShapeDtypeStruct((batch, LRU_WIDTH), F32)
    return pl.pallas_call(
        _lru_step_body,
        grid=(LRU_NTILES,),
        in_specs=[
            pl.BlockSpec(h.shape, lambda w: (0, 0)), mat(w_xy),
            tile(0), tile(LRU_NTILES), tile(2 * LRU_NTILES),
            tile(0),
            vec(CONV_W), vec(1), mat(w_bd), vec(1), vec(1), vec(1),
        ],
        out_specs=[tile(0), tile(0), tile(0)],
        out_shape=[jax.ShapeDtypeStruct((batch, LRU_WIDTH), BF16), f32_out, f32_out],
        compiler_params=_params("parallel"),
        name="lru_step",
    )(h, w_xy, conv_buf, conv_buf, conv_buf, h0, conv_w, conv_b.reshape(1, -1), w_bd,
      b_a.reshape(1, -1), b_x.reshape(1, -1), lam.reshape(1, -1))


def _merge_body(ret_ref, lru_ref, h_ref, x_ref, wr_ref, wl_ref, wg_ref, wo_ref, bg_ref, o_ref):
    h = h_ref[...]
    gate_a = jax.nn.sigmoid(jnp.dot(h, wg_ref[:, :D_MODEL], preferred_element_type=F32) + bg_ref[0:1, :])
    merged = gate_a * jnp.dot(ret_ref[...], wr_ref[...], preferred_element_type=F32)
    gate_b = jax.nn.sigmoid(jnp.dot(h, wg_ref[:, D_MODEL:], preferred_element_type=F32) + bg_ref[1:2, :])
    merged = merged + gate_b * jnp.dot(lru_ref[...], wl_ref[...], preferred_element_type=F32)
    o_ref[...] = x_ref[...] + jnp.dot(merged.astype(BF16), wo_ref[...], preferred_element_type=F32)


def _merge(ret_g, lru_g, lru_index, h, x, w_ret, w_lru, w_gab, w_o, b_gate, tm):
    t = x.shape[0]
    rows = lambda n: pl.BlockSpec((tm, n), lambda i: (i, 0))
    full = lambda a: pl.BlockSpec(a.shape, lambda i: (0, 0))
    return pl.pallas_call(
        _merge_body,
        grid=(t // tm,),
        in_specs=[rows(RET_V), pl.BlockSpec((tm, LRU_WIDTH), lru_index), rows(D_MODEL), rows(D_MODEL),
                  full(w_ret), full(w_lru), full(w_gab), full(w_o), full(b_gate)],
        out_specs=rows(D_MODEL),
        out_shape=jax.ShapeDtypeStruct((t, D_MODEL), F32),
        compiler_params=_params("parallel"),
        name="merge_out",
    )(ret_g, lru_g, h, x, w_ret, w_lru, w_gab, w_o, b_gate)


def _ffn_body(x_ref, g_ref, wgu_ref, wd_ref, o_ref):
    x = x_ref[...]
    h = (x * lax.rsqrt(jnp.mean(x * x, axis=-1, keepdims=True) + EPS) * g_ref[...]).astype(BF16)
    acc = jnp.zeros(x.shape, F32)
    for c in range(0, D_FF, FFN_CHUNK):
        gate = jnp.dot(h, wgu_ref[:, c:c + FFN_CHUNK], preferred_element_type=F32)
        up = jnp.dot(h, wgu_ref[:, D_FF + c:D_FF + c + FFN_CHUNK], preferred_element_type=F32)
        act = (jax.nn.silu(gate) * up).astype(BF16)
        acc = acc + jnp.dot(act, wd_ref[c:c + FFN_CHUNK, :], preferred_element_type=F32)
    o_ref[...] = x + acc


def _ffn(x, gain, w_gate_up, w_down, tm):
    t = x.shape[0]
    rows = pl.BlockSpec((tm, D_MODEL), lambda i: (i, 0))
    full = lambda a: pl.BlockSpec(a.shape, lambda i: (0, 0))
    return pl.pallas_call(
        _ffn_body,
        grid=(t // tm,),
        in_specs=[rows, pl.BlockSpec((1, D_MODEL), lambda i: (0, 0)), full(w_gate_up), full(w_down)],
        out_specs=rows,
        out_shape=jax.ShapeDtypeStruct((t, D_MODEL), F32),
        compiler_params=_params("parallel"),
        name="swiglu",
    )(x, gain.reshape(1, D_MODEL), w_gate_up, w_down)


def _rope_tables(pos):
    freqs = ROPE_THETA ** (-jnp.arange(ROPE_HALF, dtype=F32) / ROPE_HALF)
    ang = pos[:, None] * freqs[None, :]
    return jnp.cos(ang), jnp.sin(ang)


def _block_diag_gates(w_a, w_x):
    per_tile = LRU_TILE // LRU_BLOCK
    eye = jnp.eye(per_tile, dtype=w_a.dtype)

    def dense(w):
        w = w.reshape(LRU_NTILES, per_tile, LRU_BLOCK, LRU_BLOCK)
        return jnp.einsum("tncd,nm->tncmd", w, eye).reshape(LRU_NTILES, LRU_TILE, LRU_TILE)

    return jnp.concatenate([dense(w_a), dense(w_x)], axis=-1).astype(BF16)


def _layer_weights(l, w_in, norm_mix, gn_gain, w_ret_out, conv_w, conv_b, lru_wa, lru_ba, lru_wx, lru_bx,
                   lru_lambda, w_lru_out, b_gate, w_o, norm_ffn, w_gate_up, w_down):
    w = w_in[l].astype(BF16)
    qkv_end = 2 * RET_QK + RET_V
    gret_end = qkv_end + RET_V
    xy_end = gret_end + 2 * LRU_WIDTH
    w_xy = w[:, gret_end:xy_end].reshape(D_MODEL, 2, LRU_NTILES, LRU_TILE).transpose(2, 0, 1, 3)
    w_xy = w_xy.reshape(LRU_NTILES, D_MODEL, 2 * LRU_TILE)
    return dict(
        norm_mix=norm_mix[l], w_qkv=w[:, :qkv_end], w_gret=w[:, qkv_end:gret_end], w_xy=w_xy,
        w_gab=w[:, xy_end:], gn_gain=gn_gain[l], w_ret_out=w_ret_out[l].astype(BF16),
        conv_w=conv_w[l], conv_b=conv_b[l], w_bd=_block_diag_gates(lru_wa[l], lru_wx[l]),
        lru_ba=lru_ba[l], lru_bx=lru_bx[l], lru_lambda=lru_lambda[l],
        w_lru_out=w_lru_out[l].astype(BF16), b_gate=b_gate[l], w_o=w_o[l].astype(BF16),
        norm_ffn=norm_ffn[l], w_gate_up=w_gate_up[l].astype(BF16), w_down=w_down[l].astype(BF16))


def _retention_front(h, p, cos, sin, tm):
    qkv = _qkv_proj(h, p["w_qkv"], cos, sin, tm)
    g_ret = _matmul(h, p["w_gret"], F32, tm, 512, "gret_proj")
    return qkv, g_ret


def _token_parallel_back(x, h, ret_g, lru_g, lru_index, p, tm):
    x = _merge(ret_g, lru_g, lru_index, h, x, p["w_ret_out"], p["w_lru_out"], p["w_gab"], p["w_o"], p["b_gate"], tm)
    return _ffn(x, p["norm_ffn"], p["w_gate_up"], p["w_down"], tm)


def _prompt_layer(x, p, cos, sin, batch, seq):
    tm = 1024
    h, h_t = _rmsnorm_two_orders(x, p["norm_mix"], batch, seq, tm)
    qkv, g_ret = _retention_front(h, p, cos, sin, tm)
    ret_g, s_new = _retention_prompt(qkv, g_ret, p["gn_gain"], batch, seq)
    lru_g, h_last, x_tail = _lru_prompt(h_t.reshape(seq * batch, D_MODEL), p["w_xy"], p["conv_w"], p["conv_b"],
                                        p["w_bd"], p["lru_ba"], p["lru_bx"], p["lru_lambda"], batch, seq)
    new_buf = x_tail.reshape(CONV_W - 1, batch, LRU_WIDTH).transpose(1, 0, 2)
    tm_back = 512
    blocks = seq // tm_back
    x = _token_parallel_back(x, h, ret_g, lru_g.reshape(seq, batch * LRU_WIDTH),
                             lambda i: (i % blocks, i // blocks), p, tm_back)
    return x, s_new, h_last, new_buf


def _sample_layer(x, p, cos, sin, layer, states_ret, new_states_ret, h_lru, conv_buf):
    batch = x.shape[0]
    tm = batch
    h = _rmsnorm(x, p["norm_mix"], BF16, tm)
    qkv, g_ret = _retention_front(h, p, cos, sin, tm)
    ret_g, new_states_ret = _retention_step(qkv, g_ret, p["gn_gain"], states_ret, layer, new_states_ret)
    ret_g = ret_g.reshape(batch, RET_V).astype(BF16)
    buf2d = conv_buf.reshape(batch, (CONV_W - 1) * LRU_WIDTH)
    lru_g, h_new, x_lru = _lru_step(h, p["w_xy"], buf2d, h_lru, p["conv_w"], p["conv_b"], p["w_bd"], p["lru_ba"],
                                    p["lru_bx"], p["lru_lambda"])
    new_buf = jnp.concatenate([conv_buf[:, 1:], x_lru[:, None, :]], axis=1)
    x = _token_parallel_back(x, h, ret_g, lru_g, lambda i: (i, 0), p, tm)
    return x, new_states_ret, h_new, new_buf


def kernel(x_prompt, x_sample, state_ret, state_lru, state_conv, w_in, norm_mix, gn_gain, w_ret_out, conv_w,
           conv_b, lru_wa, lru_ba, lru_wx, lru_bx, lru_lambda, w_lru_out, b_gate, w_o, norm_ffn, w_gate_up,
           w_down, norm_final):
    bp, lp, _ = x_prompt.shape
    bs, ls, _ = x_sample.shape
    assert ls == 1 and lp % RET_ROWS == 0 and lp % LRU_STEPS == 0 and bs % STEP_SEQS == 0
    depth = w_in.shape[0]
    layers = [_layer_weights(l, w_in, norm_mix, gn_gain, w_ret_out, conv_w, conv_b, lru_wa, lru_ba, lru_wx, lru_bx,
                             lru_lambda, w_lru_out, b_gate, w_o, norm_ffn, w_gate_up, w_down)
              for l in range(depth)]

    cos_p, sin_p = _rope_tables(jnp.arange(lp, dtype=F32))
    cos_s, sin_s = _rope_tables(PAST_LEN + jnp.arange(ls, dtype=F32))
    cos_s = jnp.broadcast_to(cos_s, (bs, ROPE_HALF))
    sin_s = jnp.broadcast_to(sin_s, (bs, ROPE_HALF))

    xp = x_prompt.reshape(bp * lp, D_MODEL)
    xs = x_sample.reshape(bs * ls, D_MODEL)
    ret_p, lru_p, conv_p, lru_s, conv_s = [], [], [], [], []
    ret_s = None
    for l, p in enumerate(layers):
        xp, s, h, c = _prompt_layer(xp, p, cos_p, sin_p, bp, lp)
        ret_p.append(s), lru_p.append(h), conv_p.append(c)
        xs, ret_s, h, c = _sample_layer(xs, p, cos_s, sin_s, l, state_ret, ret_s, state_lru[l], state_conv[l])
        lru_s.append(h), conv_s.append(c)

    y_prompt = _rmsnorm(xp, norm_final, F32, 1024).reshape(bp, lp, D_MODEL)
    y_sample = _rmsnorm(xs, norm_final, F32, bs).reshape(bs, ls, D_MODEL)
    stack = lambda parts: jnp.stack(parts, 0)
    return (y_prompt, y_sample, stack(ret_p), stack(lru_p), stack(conv_p),
            ret_s, stack(lru_s), stack(conv_s))
```

```python
import jax
import jax.numpy as jnp
import numpy as np
from jax import lax
from jax.experimental import pallas as pl
from jax.experimental.pallas import tpu as pltpu

F32 = jnp.float32
BF16 = jnp.bfloat16

D_MODEL = 1024
PAST_LEN = 16384
RET_HEADS = 4
RET_DK = 256
RET_DV = 512
RET_QK = RET_HEADS * RET_DK
RET_V = RET_HEADS * RET_DV
RET_CHUNK = 128
ROPE_THETA = 10000.0
ROPE_HALF = RET_DK // 2
LRU_WIDTH = 1280
LRU_BLOCKS = 16
LRU_BLOCK = LRU_WIDTH // LRU_BLOCKS
LRU_C = 8.0
CONV_W = 4
D_FF = 2816
EPS = 1e-6

V7X_VMEM_BYTES = 64 * 1024 * 1024
VMEM_LIMIT_BYTES = V7X_VMEM_BYTES * 7 // 8
SUBLANES = 8
LANES = 128

LRU_TILE = int(np.lcm(LRU_BLOCK, LANES))
LRU_NTILES = LRU_WIDTH // LRU_TILE
LRU_ROWS = 256
LRU_STEPS = 256
RET_ROWS = 512
STEP_SEQS = 4
FFN_CHUNK = 256
QKV_TILE = 512


def _params(*semantics):
    return pltpu.CompilerParams(dimension_semantics=semantics, vmem_limit_bytes=VMEM_LIMIT_BYTES)


def _rmsnorm_body(x_ref, g_ref, o_ref):
    x = x_ref[...]
    y = x * lax.rsqrt(jnp.mean(x * x, axis=-1, keepdims=True) + EPS)
    o_ref[...] = (y * g_ref[...]).astype(o_ref.dtype)


def _rmsnorm(x, gain, out_dtype, tm):
    t, d = x.shape
    return pl.pallas_call(
        _rmsnorm_body,
        grid=(t // tm,),
        in_specs=[pl.BlockSpec((tm, d), lambda i: (i, 0)), pl.BlockSpec((1, d), lambda i: (0, 0))],
        out_specs=pl.BlockSpec((tm, d), lambda i: (i, 0)),
        out_shape=jax.ShapeDtypeStruct((t, d), out_dtype),
        compiler_params=_params("parallel"),
        name="rmsnorm",
    )(x, gain.reshape(1, d))


def _rmsnorm_two_orders_body(x_ref, g_ref, o_ref, ot_ref):
    x = x_ref[...]
    y = (x * lax.rsqrt(jnp.mean(x * x, axis=-1, keepdims=True) + EPS) * g_ref[...]).astype(o_ref.dtype)
    o_ref[...] = y
    ot_ref[...] = y


def _rmsnorm_two_orders(x, gain, batch, seq, tm):
    d = x.shape[1]
    nblk = seq // tm
    return pl.pallas_call(
        _rmsnorm_two_orders_body,
        grid=(batch, nblk),
        in_specs=[pl.BlockSpec((tm, d), lambda b, i: (b * nblk + i, 0)), pl.BlockSpec((1, d), lambda b, i: (0, 0))],
        out_specs=[pl.BlockSpec((tm, d), lambda b, i: (b * nblk + i, 0)), pl.BlockSpec((tm, d), lambda b, i: (i, b))],
        out_shape=[jax.ShapeDtypeStruct((batch * seq, d), BF16), jax.ShapeDtypeStruct((seq, batch * d), BF16)],
        compiler_params=_params("parallel", "parallel"),
        name="rmsnorm_two_orders",
    )(x, gain.reshape(1, d))


def _qkv_body(h_ref, w_ref, cos_ref, sin_ref, o_ref):
    acc = jnp.dot(h_ref[...], w_ref[...], preferred_element_type=F32)
    j = pl.program_id(1)
    n_q = RET_QK // QKV_TILE

    @pl.when(j < 2 * n_q)
    def _():
        scale = jnp.where(j < n_q, 1.0, RET_DK ** -0.5).astype(F32)
        cos = cos_ref[...]
        sin = sin_ref[...]
        for hd in range(QKV_TILE // RET_DK):
            lo = hd * RET_DK
            x1 = acc[:, lo:lo + ROPE_HALF]
            x2 = acc[:, lo + ROPE_HALF:lo + RET_DK]
            o_ref[:, lo:lo + ROPE_HALF] = ((x1 * cos - x2 * sin) * scale).astype(o_ref.dtype)
            o_ref[:, lo + ROPE_HALF:lo + RET_DK] = ((x1 * sin + x2 * cos) * scale).astype(o_ref.dtype)

    @pl.when(j >= 2 * n_q)
    def _():
        o_ref[...] = acc.astype(o_ref.dtype)


def _qkv_proj(h, w_qkv, cos, sin, tm):
    t, d = h.shape
    n = w_qkv.shape[1]
    pos_blocks = cos.shape[0] // tm
    return pl.pallas_call(
        _qkv_body,
        grid=(t // tm, n // QKV_TILE),
        in_specs=[
            pl.BlockSpec((tm, d), lambda i, j: (i, 0)),
            pl.BlockSpec((d, QKV_TILE), lambda i, j: (0, j)),
            pl.BlockSpec((tm, ROPE_HALF), lambda i, j: (i % pos_blocks, 0)),
            pl.BlockSpec((tm, ROPE_HALF), lambda i, j: (i % pos_blocks, 0)),
        ],
        out_specs=pl.BlockSpec((tm, QKV_TILE), lambda i, j: (i, j)),
        out_shape=jax.ShapeDtypeStruct((t, n), BF16),
        compiler_params=_params("parallel", "arbitrary"),
        name="qkv_proj",
    )(h, w_qkv, cos, sin)


def _matmul_body(h_ref, w_ref, o_ref):
    o_ref[...] = jnp.dot(h_ref[...], w_ref[...], preferred_element_type=F32).astype(o_ref.dtype)


def _matmul(h, w, out_dtype, tm, tn, name):
    t, d = h.shape
    n = w.shape[1]
    return pl.pallas_call(
        _matmul_body,
        grid=(t // tm, n // tn),
        in_specs=[pl.BlockSpec((tm, d), lambda i, j: (i, 0)), pl.BlockSpec((d, tn), lambda i, j: (0, j))],
        out_specs=pl.BlockSpec((tm, tn), lambda i, j: (i, j)),
        out_shape=jax.ShapeDtypeStruct((t, n), out_dtype),
        compiler_params=_params("parallel", "arbitrary"),
        name=name,
    )(h, w)


def _group_norm_gate(o, gate, gain):
    mu = jnp.mean(o, axis=-1, keepdims=True)
    var = jnp.mean(jnp.square(o - mu), axis=-1, keepdims=True)
    on = (o - mu) * lax.rsqrt(var + EPS) * gain
    return jax.nn.silu(gate) * on


def _rope_bf16(x, cos, sin, scale):
    x1 = x[:, :ROPE_HALF]
    x2 = x[:, ROPE_HALF:]
    return jnp.concatenate([(x1 * cos - x2 * sin) * scale, (x1 * sin + x2 * cos) * scale], axis=1).astype(BF16)


def _retention_body(h_ref, w_ref, cos_ref, sin_ref, gain_ref, decay_ref, qdec_ref, kdec_ref, cdec_ref,
                    *rest):
    o_ref, s_out_ref, s_ref = rest[-3:]
    blk = pl.program_id(1)

    @pl.when(blk == 0)
    def _():
        s_ref[...] = jnp.zeros_like(s_ref)

    h = h_ref[...]
    cos = cos_ref[...]
    sin = sin_ref[...]
    proj = lambda lo, n: jnp.dot(h, w_ref[:, lo:lo + n], preferred_element_type=F32)
    for hd in range(RET_HEADS):
        v_cols = slice(hd * RET_DV, (hd + 1) * RET_DV)
        q = _rope_bf16(proj(hd * RET_DK, RET_DK), cos, sin, 1.0)
        k = _rope_bf16(proj(RET_QK + hd * RET_DK, RET_DK), cos, sin, RET_DK ** -0.5)
        v = proj(2 * RET_QK + hd * RET_DV, RET_DV).astype(BF16)
        gate = proj(2 * RET_QK + RET_V + hd * RET_DV, RET_DV)
        s = s_ref[hd]
        for r0 in range(0, h.shape[0], RET_CHUNK):
            rows = slice(r0, r0 + RET_CHUNK)
            qc, kc, vc = q[rows], k[rows], v[rows]
            scores = lax.dot_general(qc, kc, (((1,), (1,)), ((), ())), preferred_element_type=F32) * decay_ref[hd]
            inner = jnp.dot(scores.astype(BF16), vc, preferred_element_type=F32)
            cross = jnp.dot(qc, s.astype(BF16), preferred_element_type=F32) * qdec_ref[hd]
            kd = (kc.astype(F32) * kdec_ref[hd]).astype(BF16)
            s = s * cdec_ref[hd] + lax.dot_general(kd, vc, (((0,), (0,)), ((), ())), preferred_element_type=F32)
            o_ref[rows, v_cols] = _group_norm_gate(inner + cross, gate[rows], gain_ref[:, v_cols]).astype(o_ref.dtype)
        s_ref[hd] = s

    @pl.when(blk == pl.num_programs(1) - 1)
    def _():
        s_out_ref[0, 0] = s_ref[...]


def _retention_tables(chunk):
    log_g = jnp.log(1.0 - 2.0 ** (-5.0 - jnp.arange(RET_HEADS, dtype=F32)))
    idx = jnp.arange(chunk, dtype=F32)
    diff = idx[:, None] - idx[None, :]
    decay = jnp.where(diff >= 0, jnp.exp(log_g[:, None, None] * jnp.maximum(diff, 0.0)), 0.0)
    q_dec = jnp.exp(log_g[:, None] * (idx[None, :] + 1.0))
    k_dec = jnp.exp(log_g[:, None] * (chunk - 1.0 - idx[None, :]))
    chunk_dec = jnp.exp(log_g * chunk)
    return decay, q_dec[:, :, None], k_dec[:, :, None], chunk_dec[:, None, None]


def _retention_prompt(h, w_qkvg, cos, sin, gain, batch, seq, depth, layer, prev_states):
    t = h.shape[0]
    decay, q_dec, k_dec, chunk_dec = _retention_tables(RET_CHUNK)
    nblk = seq // RET_ROWS
    full = lambda a: pl.BlockSpec(a.shape, lambda b, j: (0,) * a.ndim)
    pos = pl.BlockSpec((RET_ROWS, ROPE_HALF), lambda b, j: (j, 0))
    state_shape = (depth, batch, RET_HEADS, RET_DK, RET_DV)
    operands = [h, w_qkvg, cos, sin, gain.reshape(1, RET_V), decay, q_dec, k_dec, chunk_dec]
    in_specs = [pl.BlockSpec((RET_ROWS, D_MODEL), lambda b, j: (b * nblk + j, 0)), full(w_qkvg), pos, pos,
                full(operands[4]), full(decay), full(q_dec), full(k_dec), full(chunk_dec)]
    aliases = {}
    if prev_states is not None:
        aliases = {len(operands): 1}
        operands.append(prev_states)
        in_specs.append(pl.BlockSpec(memory_space=pl.ANY))
    return pl.pallas_call(
        _retention_body,
        grid=(batch, nblk),
        in_specs=in_specs,
        out_specs=[
            pl.BlockSpec((RET_ROWS, RET_V), lambda b, j: (b * nblk + j, 0)),
            pl.BlockSpec((1, 1) + state_shape[2:], lambda b, j: (layer, b, 0, 0, 0)),
        ],
        out_shape=[jax.ShapeDtypeStruct((t, RET_V), BF16), jax.ShapeDtypeStruct(state_shape, F32)],
        scratch_shapes=[pltpu.VMEM(state_shape[2:], F32)],
        input_output_aliases=aliases,
        compiler_params=_params("parallel", "arbitrary"),
        name="retention_prompt",
    )(*operands)


def _retention_step_body(qt_ref, kt_ref, q_ref, k_ref, v_ref, g_ref, gain_ref, dec_ref, s_ref, *rest):
    o_ref, s_out_ref = rest[-2:]
    n_seq = qt_ref.shape[1]
    seq_ids = lax.broadcasted_iota(jnp.int32, (n_seq, RET_DV), 0)
    for n in range(STEP_SEQS):
        seq = pl.program_id(0) * STEP_SEQS + n
        row = pl.ds(seq, 1)
        one_hot = (seq_ids == seq).astype(BF16)
        for hd in range(RET_HEADS):
            qk_cols = slice(hd * RET_DK, (hd + 1) * RET_DK)
            cols = slice(hd * RET_DV, (hd + 1) * RET_DV)
            q_spread = jnp.dot(qt_ref[qk_cols, :], one_hot, preferred_element_type=F32)
            k_spread = jnp.dot(kt_ref[qk_cols, :], one_hot, preferred_element_type=F32)
            s = s_ref[0, n, hd]
            v = v_ref[row, cols]
            decay = dec_ref[hd:hd + 1, 0:1]
            q_dec = dec_ref[hd:hd + 1, 1:2]
            k_dec = dec_ref[hd:hd + 1, 2:3]
            chunk_dec = dec_ref[hd:hd + 1, 3:4]
            scores = jnp.sum(q_ref[row, qk_cols] * k_ref[row, qk_cols], axis=1, keepdims=True) * decay
            cross = jnp.sum(q_spread * s, axis=0, keepdims=True) * q_dec
            s_out_ref[0, n, hd] = s * chunk_dec + k_spread * (k_dec * v)
            o = scores * v + cross
            o_ref[row, cols] = _group_norm_gate(o, g_ref[row, cols], gain_ref[:, cols])


def _retention_step(qkv, g_ret, gain, states, layer, prev_states):
    batch = qkv.shape[0]
    decay, q_dec, k_dec, chunk_dec = _retention_tables(1)
    dec = jnp.concatenate([decay[:, :, 0], q_dec[:, :, 0], k_dec[:, :, 0], chunk_dec[:, :, 0]], axis=1)
    q = qkv[:, :RET_QK]
    k = qkv[:, RET_QK:2 * RET_QK]
    full = lambda a: pl.BlockSpec(a.shape, lambda i: (0, 0))
    st = pl.BlockSpec((1, STEP_SEQS) + states.shape[2:], lambda i: (layer, i, 0, 0, 0))
    operands = [q.T, k.T, q.astype(F32), k.astype(F32), qkv[:, 2 * RET_QK:].astype(F32), g_ret,
                gain.reshape(1, RET_V), dec]
    in_specs = [full(a) for a in operands] + [st]
    operands.append(states)
    aliases = {}
    if prev_states is not None:
        aliases = {len(operands): 1}
        operands.append(prev_states)
        in_specs.append(pl.BlockSpec(memory_space=pl.ANY))
    return pl.pallas_call(
        _retention_step_body,
        grid=(batch // STEP_SEQS,),
        in_specs=in_specs,
        out_specs=[pl.BlockSpec((batch, RET_V), lambda i: (0, 0)), st],
        out_shape=[jax.ShapeDtypeStruct((batch, RET_V), F32), jax.ShapeDtypeStruct(states.shape, F32)],
        input_output_aliases=aliases,
        compiler_params=_params("arbitrary"),
        name="retention_step",
    )(*operands)


def _lru_gates(xc, w_ref, ba_ref, bx_ref, lam_ref):
    g = jnp.dot(xc.astype(BF16), w_ref[0], preferred_element_type=F32)
    r = jax.nn.sigmoid(g[:, :LRU_TILE] + ba_ref[...])
    i = jax.nn.sigmoid(g[:, LRU_TILE:] + bx_ref[...])
    log_a = -LRU_C * r * jax.nn.softplus(-lam_ref[...])
    a = jnp.exp(log_a)
    th = jnp.tanh(log_a)
    u = jnp.sqrt(-2.0 * th / (1.0 - th)) * (i * xc)
    return a, u


def _lru_prompt_body(h_ref, wxy_ref, cw_ref, cb_ref, w_ref, ba_ref, bx_ref, lam_ref,
                     o_ref, hl_ref, xt_ref, xpad_ref, state_ref):
    rows = h_ref.shape[0]
    tw = o_ref.shape[1]
    tail = (CONV_W - 1) * SUBLANES
    blk = pl.program_id(1)

    @pl.when(blk == 0)
    def _():
        xpad_ref[0:tail, :] = jnp.zeros((tail, tw), F32)
        state_ref[...] = jnp.zeros_like(state_ref)

    h = state_ref[...]
    for r0 in range(0, rows, LRU_ROWS):
        xy = jnp.dot(h_ref[r0:r0 + LRU_ROWS, :], wxy_ref[0], preferred_element_type=F32)
        xpad_ref[tail:tail + LRU_ROWS, :] = xy[:, :tw]
        xc = cb_ref[...]
        for j in range(CONV_W):
            xc = xc + xpad_ref[j * SUBLANES:j * SUBLANES + LRU_ROWS, :] * cw_ref[j:j + 1, :]
        a, u = _lru_gates(xc, w_ref, ba_ref, bx_ref, lam_ref)
        steps = []
        for g in range(0, LRU_ROWS, SUBLANES):
            h = a[g:g + SUBLANES, :] * h + u[g:g + SUBLANES, :]
            steps.append(h)
        o_ref[r0:r0 + LRU_ROWS, :] = (jax.nn.gelu(xy[:, tw:]) * jnp.concatenate(steps, axis=0)).astype(o_ref.dtype)
        xpad_ref[0:tail, :] = xpad_ref[LRU_ROWS:LRU_ROWS + tail, :]
    state_ref[...] = h

    @pl.when(blk == pl.num_programs(1) - 1)
    def _():
        hl_ref[...] = h
        xt_ref[...] = xpad_ref[0:tail, :]


def _lru_specs(rows, index):
    return pl.BlockSpec((rows, LRU_TILE), index)


def _lru_prompt(h_t, w_xy, conv_w, conv_b, w_bd, b_a, b_x, lam, batch, seq):
    assert batch == SUBLANES
    rows = LRU_STEPS * batch
    tail = (CONV_W - 1) * batch
    vec = lambda r: _lru_specs(r, lambda w, t: (0, w))
    mat = lambda a: pl.BlockSpec((1,) + a.shape[1:], lambda w, t: (w, 0, 0))
    return pl.pallas_call(
        _lru_prompt_body,
        grid=(LRU_NTILES, seq // LRU_STEPS),
        in_specs=[
            pl.BlockSpec((rows, D_MODEL), lambda w, t: (t, 0)),
            mat(w_xy), vec(CONV_W), vec(1), mat(w_bd), vec(1), vec(1), vec(1),
        ],
        out_specs=[_lru_specs(rows, lambda w, t: (t, w)), vec(batch), vec(tail)],
        out_shape=[
            jax.ShapeDtypeStruct((seq * batch, LRU_WIDTH), BF16),
            jax.ShapeDtypeStruct((batch, LRU_WIDTH), F32),
            jax.ShapeDtypeStruct((tail, LRU_WIDTH), F32),
        ],
        scratch_shapes=[pltpu.VMEM((tail + LRU_ROWS, LRU_TILE), F32), pltpu.VMEM((batch, LRU_TILE), F32)],
        compiler_params=_params("parallel", "arbitrary"),
        name="lru_prompt",
    )(h_t, w_xy, conv_w, conv_b.reshape(1, -1), w_bd, b_a.reshape(1, -1), b_x.reshape(1, -1), lam.reshape(1, -1))


def _lru_step_body(h_ref, wxy_ref, b0_ref, b1_ref, b2_ref, h0_ref, cw_ref, cb_ref, w_ref, ba_ref, bx_ref, lam_ref,
                   o_ref, hn_ref, x_ref):
    tw = o_ref.shape[1]
    xy = jnp.dot(h_ref[...], wxy_ref[0], preferred_element_type=F32)
    x = xy[:, :tw]
    x_ref[...] = x
    taps = (b0_ref[...], b1_ref[...], b2_ref[...], x)
    xc = cb_ref[...]
    for j in range(CONV_W):
        xc = xc + taps[j] * cw_ref[j:j + 1, :]
    a, u = _lru_gates(xc, w_ref, ba_ref, bx_ref, lam_ref)
    h = u + a * h0_ref[...]
    hn_ref[...] = h
    o_ref[...] = (jax.nn.gelu(xy[:, tw:]) * h).astype(o_ref.dtype)


def _lru_step(h, w_xy, conv_buf, h0, conv_w, conv_b, w_bd, b_a, b_x, lam):
    batch = h.shape[0]
    tile = lambda off: _lru_specs(batch, lambda w: (0, off + w))
    vec = lambda rows: _lru_specs(rows, lambda w: (0, w))
    mat = lambda a: pl.BlockSpec((1,) + a.shape[1:], lambda w: (w, 0, 0))
    f32_out = jax.ShapeDtypeStruct((batch, LRU_WIDTH), F32)
    return pl.pallas_call(
        _lru_step_body,
        grid=(LRU_NTILES,),
        in_specs=[
            pl.BlockSpec(h.shape, lambda w: (0, 0)), mat(w_xy),
            tile(0), tile(LRU_NTILES), tile(2 * LRU_NTILES),
            tile(0),
            vec(CONV_W), vec(1), mat(w_bd), vec(1), vec(1), vec(1),
        ],
        out_specs=[tile(0), tile(0), tile(0)],
        out_shape=[jax.ShapeDtypeStruct((batch, LRU_WIDTH), BF16), f32_out, f32_out],
        compiler_params=_params("parallel"),
        name="lru_step",
    )(h, w_xy, conv_buf, conv_buf, conv_buf, h0, conv_w, conv_b.reshape(1, -1), w_bd,
      b_a.reshape(1, -1), b_x.reshape(1, -1), lam.reshape(1, -1))


def _merge_body(ret_ref, lru_ref, h_ref, x_ref, wr_ref, wl_ref, wg_ref, wo_ref, bg_ref, o_ref):
    h = h_ref[...]
    gate_a = jax.nn.sigmoid(jnp.dot(h, wg_ref[:, :D_MODEL], preferred_element_type=F32) + bg_ref[0:1, :])
    merged = gate_a * jnp.dot(ret_ref[...], wr_ref[...], preferred_element_type=F32)
    gate_b = jax.nn.sigmoid(jnp.dot(h, wg_ref[:, D_MODEL:], preferred_element_type=F32) + bg_ref[1:2, :])
    merged = merged + gate_b * jnp.dot(lru_ref[...], wl_ref[...], preferred_element_type=F32)
    o_ref[...] = x_ref[...] + jnp.dot(merged.astype(BF16), wo_ref[...], preferred_element_type=F32)


def _merge(ret_g, lru_g, lru_index, h, x, w_ret, w_lru, w_gab, w_o, b_gate, tm):
    t = x.shape[0]
    rows = lambda n: pl.BlockSpec((tm, n), lambda i: (i, 0))
    full = lambda a: pl.BlockSpec(a.shape, lambda i: (0, 0))
    return pl.pallas_call(
        _merge_body,
        grid=(t // tm,),
        in_specs=[rows(RET_V), pl.BlockSpec((tm, LRU_WIDTH), lru_index), rows(D_MODEL), rows(D_MODEL),
                  full(w_ret), full(w_lru), full(w_gab), full(w_o), full(b_gate)],
        out_specs=rows(D_MODEL),
        out_shape=jax.ShapeDtypeStruct((t, D_MODEL), F32),
        compiler_params=_params("parallel"),
        name="merge_out",
    )(ret_g, lru_g, h, x, w_ret, w_lru, w_gab, w_o, b_gate)


def _ffn_body(x_ref, g_ref, wgu_ref, wd_ref, o_ref):
    x = x_ref[...]
    h = (x * lax.rsqrt(jnp.mean(x * x, axis=-1, keepdims=True) + EPS) * g_ref[...]).astype(BF16)
    acc = jnp.zeros(x.shape, F32)
    for c in range(0, D_FF, FFN_CHUNK):
        gate = jnp.dot(h, wgu_ref[:, c:c + FFN_CHUNK], preferred_element_type=F32)
        up = jnp.dot(h, wgu_ref[:, D_FF + c:D_FF + c + FFN_CHUNK], preferred_element_type=F32)
        act = (jax.nn.silu(gate) * up).astype(BF16)
        acc = acc + jnp.dot(act, wd_ref[c:c + FFN_CHUNK, :], preferred_element_type=F32)
    o_ref[...] = x + acc


def _ffn(x, gain, w_gate_up, w_down, tm):
    t = x.shape[0]
    rows = pl.BlockSpec((tm, D_MODEL), lambda i: (i, 0))
    full = lambda a: pl.BlockSpec(a.shape, lambda i: (0, 0))
    return pl.pallas_call(
        _ffn_body,
        grid=(t // tm,),
        in_specs=[rows, pl.BlockSpec((1, D_MODEL), lambda i: (0, 0)), full(w_gate_up), full(w_down)],
        out_specs=rows,
        out_shape=jax.ShapeDtypeStruct((t, D_MODEL), F32),
        compiler_params=_params("parallel"),
        name="swiglu",
    )(x, gain.reshape(1, D_MODEL), w_gate_up, w_down)


def _rope_tables(pos):
    freqs = ROPE_THETA ** (-jnp.arange(ROPE_HALF, dtype=F32) / ROPE_HALF)
    ang = pos[:, None] * freqs[None, :]
    return jnp.cos(ang), jnp.sin(ang)


def _block_diag_gates(w_a, w_x):
    per_tile = LRU_TILE // LRU_BLOCK
    eye = jnp.eye(per_tile, dtype=w_a.dtype)

    def dense(w):
        w = w.reshape(LRU_NTILES, per_tile, LRU_BLOCK, LRU_BLOCK)
        return jnp.einsum("tncd,nm->tncmd", w, eye).reshape(LRU_NTILES, LRU_TILE, LRU_TILE)

    return jnp.concatenate([dense(w_a), dense(w_x)], axis=-1).astype(BF16)


def _layer_weights(l, w_in, norm_mix, gn_gain, w_ret_out, conv_w, conv_b, lru_wa, lru_ba, lru_wx, lru_bx,
                   lru_lambda, w_lru_out, b_gate, w_o, norm_ffn, w_gate_up, w_down):
    w = w_in[l].astype(BF16)
    qkv_end = 2 * RET_QK + RET_V
    gret_end = qkv_end + RET_V
    xy_end = gret_end + 2 * LRU_WIDTH
    w_xy = w[:, gret_end:xy_end].reshape(D_MODEL, 2, LRU_NTILES, LRU_TILE).transpose(2, 0, 1, 3)
    w_xy = w_xy.reshape(LRU_NTILES, D_MODEL, 2 * LRU_TILE)
    return dict(
        norm_mix=norm_mix[l], w_qkv=w[:, :qkv_end], w_gret=w[:, qkv_end:gret_end], w_qkvg=w[:, :gret_end],
        w_xy=w_xy, w_gab=w[:, xy_end:], gn_gain=gn_gain[l], w_ret_out=w_ret_out[l].astype(BF16),
        conv_w=conv_w[l], conv_b=conv_b[l], w_bd=_block_diag_gates(lru_wa[l], lru_wx[l]),
        lru_ba=lru_ba[l], lru_bx=lru_bx[l], lru_lambda=lru_lambda[l],
        w_lru_out=w_lru_out[l].astype(BF16), b_gate=b_gate[l], w_o=w_o[l].astype(BF16),
        norm_ffn=norm_ffn[l], w_gate_up=w_gate_up[l].astype(BF16), w_down=w_down[l].astype(BF16))


def _retention_front(h, p, cos, sin, tm):
    qkv = _qkv_proj(h, p["w_qkv"], cos, sin, tm)
    g_ret = _matmul(h, p["w_gret"], F32, tm, 512, "gret_proj")
    return qkv, g_ret


def _token_parallel_back(x, h, ret_g, lru_g, lru_index, p, tm):
    x = _merge(ret_g, lru_g, lru_index, h, x, p["w_ret_out"], p["w_lru_out"], p["w_gab"], p["w_o"], p["b_gate"], tm)
    return _ffn(x, p["norm_ffn"], p["w_gate_up"], p["w_down"], tm)


def _prompt_layer(x, p, cos, sin, batch, seq, depth, layer, states_ret):
    tm = 1024
    h, h_t = _rmsnorm_two_orders(x, p["norm_mix"], batch, seq, tm)
    ret_g, s_new = _retention_prompt(h, p["w_qkvg"], cos, sin, p["gn_gain"], batch, seq, depth, layer, states_ret)
    lru_g, h_last, x_tail = _lru_prompt(h_t.reshape(seq * batch, D_MODEL), p["w_xy"], p["conv_w"], p["conv_b"],
                                        p["w_bd"], p["lru_ba"], p["lru_bx"], p["lru_lambda"], batch, seq)
    new_buf = x_tail.reshape(CONV_W - 1, batch, LRU_WIDTH).transpose(1, 0, 2)
    tm_back = 512
    blocks = seq // tm_back
    x = _token_parallel_back(x, h, ret_g, lru_g.reshape(seq, batch * LRU_WIDTH),
                             lambda i: (i % blocks, i // blocks), p, tm_back)
    return x, s_new, h_last, new_buf


def _sample_layer(x, p, cos, sin, layer, states_ret, new_states_ret, h_lru, conv_buf):
    batch = x.shape[0]
    tm = batch
    h = _rmsnorm(x, p["norm_mix"], BF16, tm)
    qkv, g_ret = _retention_front(h, p, cos, sin, tm)
    ret_g, new_states_ret = _retention_step(qkv, g_ret, p["gn_gain"], states_ret, layer, new_states_ret)
    ret_g = ret_g.astype(BF16)
    buf2d = conv_buf.reshape(batch, (CONV_W - 1) * LRU_WIDTH)
    lru_g, h_new, x_lru = _lru_step(h, p["w_xy"], buf2d, h_lru, p["conv_w"], p["conv_b"], p["w_bd"], p["lru_ba"],
                                    p["lru_bx"], p["lru_lambda"])
    new_buf = jnp.concatenate([conv_buf[:, 1:], x_lru[:, None, :]], axis=1)
    x = _token_parallel_back(x, h, ret_g, lru_g, lambda i: (i, 0), p, tm)
    return x, new_states_ret, h_new, new_buf


def kernel(x_prompt, x_sample, state_ret, state_lru, state_conv, w_in, norm_mix, gn_gain, w_ret_out, conv_w,
           conv_b, lru_wa, lru_ba, lru_wx, lru_bx, lru_lambda, w_lru_out, b_gate, w_o, norm_ffn, w_gate_up,
           w_down, norm_final):
    bp, lp, _ = x_prompt.shape
    bs, ls, _ = x_sample.shape
    assert ls == 1 and lp % RET_ROWS == 0 and lp % LRU_STEPS == 0 and bs % STEP_SEQS == 0
    depth = w_in.shape[0]
    layers = [_layer_weights(l, w_in, norm_mix, gn_gain, w_ret_out, conv_w, conv_b, lru_wa, lru_ba, lru_wx, lru_bx,
                             lru_lambda, w_lru_out, b_gate, w_o, norm_ffn, w_gate_up, w_down)
              for l in range(depth)]

    cos_p, sin_p = _rope_tables(jnp.arange(lp, dtype=F32))
    cos_s, sin_s = _rope_tables(PAST_LEN + jnp.arange(ls, dtype=F32))
    cos_s = jnp.broadcast_to(cos_s, (bs, ROPE_HALF))
    sin_s = jnp.broadcast_to(sin_s, (bs, ROPE_HALF))

    xp = x_prompt.reshape(bp * lp, D_MODEL)
    xs = x_sample.reshape(bs * ls, D_MODEL)
    lru_p, conv_p, lru_s, conv_s = [], [], [], []
    ret_p = ret_s = None
    for l, p in enumerate(layers):
        xp, ret_p, h, c = _prompt_layer(xp, p, cos_p, sin_p, bp, lp, depth, l, ret_p)
        lru_p.append(h), conv_p.append(c)
        xs, ret_s, h, c = _sample_layer(xs, p, cos_s, sin_s, l, state_ret, ret_s, state_lru[l], state_conv[l])
        lru_s.append(h), conv_s.append(c)

    y_prompt = _rmsnorm(xp, norm_final, F32, 1024).reshape(bp, lp, D_MODEL)
    y_sample = _rmsnorm(xs, norm_final, F32, bs).reshape(bs, ls, D_MODEL)
    stack = lambda parts: jnp.stack(parts, 0)
    return (y_prompt, y_sample, ret_p, stack(lru_p), stack(conv_p), ret_s, stack(lru_s), stack(conv_s))
```

```python
import jax
import jax.numpy as jnp
import numpy as np
from jax import lax
from jax.experimental import pallas as pl
from jax.experimental.pallas import tpu as pltpu

F32 = jnp.float32
BF16 = jnp.bfloat16

D_MODEL = 1024
PAST_LEN = 16384
RET_HEADS = 4
RET_DK = 256
RET_DV = 512
RET_QK = RET_HEADS * RET_DK
RET_V = RET_HEADS * RET_DV
RET_CHUNK = 128
ROPE_THETA = 10000.0
ROPE_HALF = RET_DK // 2
LRU_WIDTH = 1280
LRU_BLOCKS = 16
LRU_BLOCK = LRU_WIDTH // LRU_BLOCKS
LRU_C = 8.0
CONV_W = 4
D_FF = 2816
EPS = 1e-6

V7X_VMEM_BYTES = 64 * 1024 * 1024
VMEM_LIMIT_BYTES = V7X_VMEM_BYTES * 7 // 8
SUBLANES = 8
LANES = 128

LRU_TILE = int(np.lcm(LRU_BLOCK, LANES))
LRU_NTILES = LRU_WIDTH // LRU_TILE
LRU_ROWS = 256
LRU_STEPS = 256
LRU_PITCH = LRU_STEPS + SUBLANES
RET_ROWS = 512
STEP_SEQS = 4
FFN_CHUNK = 256
QKV_TILE = 512


def _params(*semantics):
    return pltpu.CompilerParams(dimension_semantics=semantics, vmem_limit_bytes=VMEM_LIMIT_BYTES)


def _rmsnorm_body(x_ref, g_ref, o_ref):
    x = x_ref[...]
    y = x * lax.rsqrt(jnp.mean(x * x, axis=-1, keepdims=True) + EPS)
    o_ref[...] = (y * g_ref[...]).astype(o_ref.dtype)


def _rmsnorm(x, gain, out_dtype, tm):
    t, d = x.shape
    return pl.pallas_call(
        _rmsnorm_body,
        grid=(t // tm,),
        in_specs=[pl.BlockSpec((tm, d), lambda i: (i, 0)), pl.BlockSpec((1, d), lambda i: (0, 0))],
        out_specs=pl.BlockSpec((tm, d), lambda i: (i, 0)),
        out_shape=jax.ShapeDtypeStruct((t, d), out_dtype),
        compiler_params=_params("parallel"),
        name="rmsnorm",
    )(x, gain.reshape(1, d))


def _qkv_body(h_ref, w_ref, cos_ref, sin_ref, o_ref):
    acc = jnp.dot(h_ref[...], w_ref[...], preferred_element_type=F32)
    j = pl.program_id(1)
    n_q = RET_QK // QKV_TILE

    @pl.when(j < 2 * n_q)
    def _():
        scale = jnp.where(j < n_q, 1.0, RET_DK ** -0.5).astype(F32)
        cos = cos_ref[...]
        sin = sin_ref[...]
        for hd in range(QKV_TILE // RET_DK):
            lo = hd * RET_DK
            x1 = acc[:, lo:lo + ROPE_HALF]
            x2 = acc[:, lo + ROPE_HALF:lo + RET_DK]
            o_ref[:, lo:lo + ROPE_HALF] = ((x1 * cos - x2 * sin) * scale).astype(o_ref.dtype)
            o_ref[:, lo + ROPE_HALF:lo + RET_DK] = ((x1 * sin + x2 * cos) * scale).astype(o_ref.dtype)

    @pl.when(j >= 2 * n_q)
    def _():
        o_ref[...] = acc.astype(o_ref.dtype)


def _qkv_proj(h, w_qkv, cos, sin, tm):
    t, d = h.shape
    n = w_qkv.shape[1]
    pos_blocks = cos.shape[0] // tm
    return pl.pallas_call(
        _qkv_body,
        grid=(t // tm, n // QKV_TILE),
        in_specs=[
            pl.BlockSpec((tm, d), lambda i, j: (i, 0)),
            pl.BlockSpec((d, QKV_TILE), lambda i, j: (0, j)),
            pl.BlockSpec((tm, ROPE_HALF), lambda i, j: (i % pos_blocks, 0)),
            pl.BlockSpec((tm, ROPE_HALF), lambda i, j: (i % pos_blocks, 0)),
        ],
        out_specs=pl.BlockSpec((tm, QKV_TILE), lambda i, j: (i, j)),
        out_shape=jax.ShapeDtypeStruct((t, n), BF16),
        compiler_params=_params("parallel", "arbitrary"),
        name="qkv_proj",
    )(h, w_qkv, cos, sin)


def _matmul_body(h_ref, w_ref, o_ref):
    o_ref[...] = jnp.dot(h_ref[...], w_ref[...], preferred_element_type=F32).astype(o_ref.dtype)


def _matmul(h, w, out_dtype, tm, tn, name):
    t, d = h.shape
    n = w.shape[1]
    return pl.pallas_call(
        _matmul_body,
        grid=(t // tm, n // tn),
        in_specs=[pl.BlockSpec((tm, d), lambda i, j: (i, 0)), pl.BlockSpec((d, tn), lambda i, j: (0, j))],
        out_specs=pl.BlockSpec((tm, tn), lambda i, j: (i, j)),
        out_shape=jax.ShapeDtypeStruct((t, n), out_dtype),
        compiler_params=_params("parallel", "arbitrary"),
        name=name,
    )(h, w)


def _group_norm_gate(o, gate, gain):
    mu = jnp.mean(o, axis=-1, keepdims=True)
    var = jnp.mean(jnp.square(o - mu), axis=-1, keepdims=True)
    on = (o - mu) * lax.rsqrt(var + EPS) * gain
    return jax.nn.silu(gate) * on


def _rope_bf16(x, cos, sin, scale):
    x1 = x[:, :ROPE_HALF]
    x2 = x[:, ROPE_HALF:]
    return jnp.concatenate([(x1 * cos - x2 * sin) * scale, (x1 * sin + x2 * cos) * scale], axis=1).astype(BF16)


def _retention_body(h_ref, w_ref, cos_ref, sin_ref, gain_ref, decay_ref, qdec_ref, kdec_ref, cdec_ref,
                    *rest):
    o_ref, s_out_ref, s_ref = rest[-3:]
    blk = pl.program_id(1)

    @pl.when(blk == 0)
    def _():
        s_ref[...] = jnp.zeros_like(s_ref)

    h = h_ref[...]
    cos = cos_ref[...]
    sin = sin_ref[...]
    proj = lambda lo, n: jnp.dot(h, w_ref[:, lo:lo + n], preferred_element_type=F32)
    for hd in range(RET_HEADS):
        v_cols = slice(hd * RET_DV, (hd + 1) * RET_DV)
        q = _rope_bf16(proj(hd * RET_DK, RET_DK), cos, sin, 1.0)
        k = _rope_bf16(proj(RET_QK + hd * RET_DK, RET_DK), cos, sin, RET_DK ** -0.5)
        v = proj(2 * RET_QK + hd * RET_DV, RET_DV).astype(BF16)
        gate = proj(2 * RET_QK + RET_V + hd * RET_DV, RET_DV)
        s = s_ref[hd]
        for r0 in range(0, h.shape[0], RET_CHUNK):
            rows = slice(r0, r0 + RET_CHUNK)
            qc, kc, vc = q[rows], k[rows], v[rows]
            scores = lax.dot_general(qc, kc, (((1,), (1,)), ((), ())), preferred_element_type=F32) * decay_ref[hd]
            inner = jnp.dot(scores.astype(BF16), vc, preferred_element_type=F32)
            cross = jnp.dot(qc, s.astype(BF16), preferred_element_type=F32) * qdec_ref[hd]
            kd = (kc.astype(F32) * kdec_ref[hd]).astype(BF16)
            s = s * cdec_ref[hd] + lax.dot_general(kd, vc, (((0,), (0,)), ((), ())), preferred_element_type=F32)
            o_ref[rows, v_cols] = _group_norm_gate(inner + cross, gate[rows], gain_ref[:, v_cols]).astype(o_ref.dtype)
        s_ref[hd] = s

    @pl.when(blk == pl.num_programs(1) - 1)
    def _():
        s_out_ref[0, 0] = s_ref[...]


def _retention_tables(chunk):
    log_g = jnp.log(1.0 - 2.0 ** (-5.0 - jnp.arange(RET_HEADS, dtype=F32)))
    idx = jnp.arange(chunk, dtype=F32)
    diff = idx[:, None] - idx[None, :]
    decay = jnp.where(diff >= 0, jnp.exp(log_g[:, None, None] * jnp.maximum(diff, 0.0)), 0.0)
    q_dec = jnp.exp(log_g[:, None] * (idx[None, :] + 1.0))
    k_dec = jnp.exp(log_g[:, None] * (chunk - 1.0 - idx[None, :]))
    chunk_dec = jnp.exp(log_g * chunk)
    return decay, q_dec[:, :, None], k_dec[:, :, None], chunk_dec[:, None, None]


def _retention_prompt(h, w_qkvg, cos, sin, gain, batch, seq, depth, layer, prev_states):
    t = h.shape[0]
    decay, q_dec, k_dec, chunk_dec = _retention_tables(RET_CHUNK)
    nblk = seq // RET_ROWS
    full = lambda a: pl.BlockSpec(a.shape, lambda b, j: (0,) * a.ndim)
    pos = pl.BlockSpec((RET_ROWS, ROPE_HALF), lambda b, j: (j, 0))
    state_shape = (depth, batch, RET_HEADS, RET_DK, RET_DV)
    operands = [h, w_qkvg, cos, sin, gain.reshape(1, RET_V), decay, q_dec, k_dec, chunk_dec]
    in_specs = [pl.BlockSpec((RET_ROWS, D_MODEL), lambda b, j: (b * nblk + j, 0)), full(w_qkvg), pos, pos,
                full(operands[4]), full(decay), full(q_dec), full(k_dec), full(chunk_dec)]
    aliases = {}
    if prev_states is not None:
        aliases = {len(operands): 1}
        operands.append(prev_states)
        in_specs.append(pl.BlockSpec(memory_space=pl.ANY))
    return pl.pallas_call(
        _retention_body,
        grid=(batch, nblk),
        in_specs=in_specs,
        out_specs=[
            pl.BlockSpec((RET_ROWS, RET_V), lambda b, j: (b * nblk + j, 0)),
            pl.BlockSpec((1, 1) + state_shape[2:], lambda b, j: (layer, b, 0, 0, 0)),
        ],
        out_shape=[jax.ShapeDtypeStruct((t, RET_V), BF16), jax.ShapeDtypeStruct(state_shape, F32)],
        scratch_shapes=[pltpu.VMEM(state_shape[2:], F32)],
        input_output_aliases=aliases,
        compiler_params=_params("parallel", "arbitrary"),
        name="retention_prompt",
    )(*operands)


def _retention_step_body(qt_ref, kt_ref, q_ref, k_ref, v_ref, g_ref, gain_ref, dec_ref, s_ref, *rest):
    o_ref, s_out_ref = rest[-2:]
    n_seq = qt_ref.shape[1]
    seq_ids = lax.broadcasted_iota(jnp.int32, (n_seq, RET_DV), 0)
    for n in range(STEP_SEQS):
        seq = pl.program_id(0) * STEP_SEQS + n
        row = pl.ds(seq, 1)
        one_hot = (seq_ids == seq).astype(BF16)
        for hd in range(RET_HEADS):
            qk_cols = slice(hd * RET_DK, (hd + 1) * RET_DK)
            cols = slice(hd * RET_DV, (hd + 1) * RET_DV)
            q_spread = jnp.dot(qt_ref[qk_cols, :], one_hot, preferred_element_type=F32)
            k_spread = jnp.dot(kt_ref[qk_cols, :], one_hot, preferred_element_type=F32)
            s = s_ref[0, n, hd]
            v = v_ref[row, cols]
            decay = dec_ref[hd:hd + 1, 0:1]
            q_dec = dec_ref[hd:hd + 1, 1:2]
            k_dec = dec_ref[hd:hd + 1, 2:3]
            chunk_dec = dec_ref[hd:hd + 1, 3:4]
            scores = jnp.sum(q_ref[row, qk_cols] * k_ref[row, qk_cols], axis=1, keepdims=True) * decay
            cross = jnp.sum(q_spread * s, axis=0, keepdims=True) * q_dec
            s_out_ref[0, n, hd] = s * chunk_dec + k_spread * (k_dec * v)
            o = scores * v + cross
            o_ref[row, cols] = _group_norm_gate(o, g_ref[row, cols], gain_ref[:, cols])


def _retention_step(qkv, g_ret, gain, states, layer, prev_states):
    batch = qkv.shape[0]
    decay, q_dec, k_dec, chunk_dec = _retention_tables(1)
    dec = jnp.concatenate([decay[:, :, 0], q_dec[:, :, 0], k_dec[:, :, 0], chunk_dec[:, :, 0]], axis=1)
    q = qkv[:, :RET_QK]
    k = qkv[:, RET_QK:2 * RET_QK]
    full = lambda a: pl.BlockSpec(a.shape, lambda i: (0, 0))
    st = pl.BlockSpec((1, STEP_SEQS) + states.shape[2:], lambda i: (layer, i, 0, 0, 0))
    operands = [q.T, k.T, q.astype(F32), k.astype(F32), qkv[:, 2 * RET_QK:].astype(F32), g_ret,
                gain.reshape(1, RET_V), dec]
    in_specs = [full(a) for a in operands] + [st]
    operands.append(states)
    aliases = {}
    if prev_states is not None:
        aliases = {len(operands): 1}
        operands.append(prev_states)
        in_specs.append(pl.BlockSpec(memory_space=pl.ANY))
    return pl.pallas_call(
        _retention_step_body,
        grid=(batch // STEP_SEQS,),
        in_specs=in_specs,
        out_specs=[pl.BlockSpec((batch, RET_V), lambda i: (0, 0)), st],
        out_shape=[jax.ShapeDtypeStruct((batch, RET_V), F32), jax.ShapeDtypeStruct(states.shape, F32)],
        input_output_aliases=aliases,
        compiler_params=_params("arbitrary"),
        name="retention_step",
    )(*operands)


def _lru_gates(xc, w_ref, ba_ref, bx_ref, lam_ref):
    g = jnp.dot(xc.astype(BF16), w_ref[0], preferred_element_type=F32)
    r = jax.nn.sigmoid(g[:, :LRU_TILE] + ba_ref[...])
    i = jax.nn.sigmoid(g[:, LRU_TILE:] + bx_ref[...])
    log_a = -LRU_C * r * jax.nn.softplus(-lam_ref[...])
    a = jnp.exp(log_a)
    th = jnp.tanh(log_a)
    u = jnp.sqrt(-2.0 * th / (1.0 - th)) * (i * xc)
    return a, u


def _lru_prompt_body(h_ref, wxy_ref, cw_ref, cb_ref, w_ref, ba_ref, bx_ref, lam_ref,
                     o_ref, hl_ref, xt_ref, xs_ref, hs_ref, y_ref, xpad_ref, state_ref):
    n_seq, steps, _ = h_ref.shape
    tw = o_ref.shape[2]
    slabs = tw // LANES
    tail = (CONV_W - 1) * SUBLANES
    sub_steps = LRU_ROWS // SUBLANES
    blk = pl.program_id(1)

    @pl.when(blk == 0)
    def _():
        xpad_ref[0:tail, :] = jnp.zeros((tail, tw), F32)
        state_ref[...] = jnp.zeros_like(state_ref)

    for b in range(n_seq):
        xy = jnp.dot(h_ref[b], wxy_ref[0], preferred_element_type=F32)
        for s in range(slabs):
            xs_ref[s, b * LRU_PITCH:b * LRU_PITCH + steps, :] = xy[:, s * LANES:(s + 1) * LANES]
        y_ref[b] = xy[:, tw:]

    h = state_ref[...]
    for t0 in range(0, steps, sub_steps):
        for i in range(sub_steps):
            for s in range(slabs):
                xpad_ref[tail + i * SUBLANES:tail + (i + 1) * SUBLANES, s * LANES:(s + 1) * LANES] = (
                    xs_ref.at[s][pl.ds(t0 + i, n_seq, stride=LRU_PITCH), :])
        xc = cb_ref[...]
        for j in range(CONV_W):
            xc = xc + xpad_ref[j * SUBLANES:j * SUBLANES + LRU_ROWS, :] * cw_ref[j:j + 1, :]
        a, u = _lru_gates(xc, w_ref, ba_ref, bx_ref, lam_ref)
        for i in range(sub_steps):
            g = i * SUBLANES
            h = a[g:g + SUBLANES, :] * h + u[g:g + SUBLANES, :]
            for s in range(slabs):
                hs_ref.at[s][pl.ds(t0 + i, n_seq, stride=LRU_PITCH), :] = h[:, s * LANES:(s + 1) * LANES]
        xpad_ref[0:tail, :] = xpad_ref[LRU_ROWS:LRU_ROWS + tail, :]
    state_ref[...] = h

    for b in range(n_seq):
        plane = slice(b * LRU_PITCH, b * LRU_PITCH + steps)
        h_seq = jnp.concatenate([hs_ref[s, plane, :] for s in range(slabs)], axis=1)
        o_ref[b] = (jax.nn.gelu(y_ref[b]) * h_seq).astype(o_ref.dtype)

    @pl.when(blk == pl.num_programs(1) - 1)
    def _():
        hl_ref[...] = h
        xt_ref[...] = xpad_ref[0:tail, :]


def _lru_specs(rows, index):
    return pl.BlockSpec((rows, LRU_TILE), index)


def _lru_prompt(h, w_xy, conv_w, conv_b, w_bd, b_a, b_x, lam):
    batch, seq, d = h.shape
    assert batch == SUBLANES
    tail = (CONV_W - 1) * batch
    vec = lambda r: _lru_specs(r, lambda w, t: (0, w))
    mat = lambda a: pl.BlockSpec((1,) + a.shape[1:], lambda w, t: (w, 0, 0))
    staging = pltpu.VMEM((LRU_TILE // LANES, batch * LRU_PITCH, LANES), F32)
    return pl.pallas_call(
        _lru_prompt_body,
        grid=(LRU_NTILES, seq // LRU_STEPS),
        in_specs=[
            pl.BlockSpec((batch, LRU_STEPS, d), lambda w, t: (0, t, 0)),
            mat(w_xy), vec(CONV_W), vec(1), mat(w_bd), vec(1), vec(1), vec(1),
        ],
        out_specs=[pl.BlockSpec((batch, LRU_STEPS, LRU_TILE), lambda w, t: (0, t, w)), vec(batch), vec(tail)],
        out_shape=[
            jax.ShapeDtypeStruct((batch, seq, LRU_WIDTH), BF16),
            jax.ShapeDtypeStruct((batch, LRU_WIDTH), F32),
            jax.ShapeDtypeStruct((tail, LRU_WIDTH), F32),
        ],
        scratch_shapes=[
            staging, staging,
            pltpu.VMEM((batch, LRU_STEPS, LRU_TILE), F32),
            pltpu.VMEM((tail + LRU_ROWS, LRU_TILE), F32),
            pltpu.VMEM((batch, LRU_TILE), F32),
        ],
        compiler_params=_params("parallel", "arbitrary"),
        name="lru_prompt",
    )(h, w_xy, conv_w, conv_b.reshape(1, -1), w_bd, b_a.reshape(1, -1), b_x.reshape(1, -1), lam.reshape(1, -1))


def _lru_step_body(h_ref, wxy_ref, b0_ref, b1_ref, b2_ref, h0_ref, cw_ref, cb_ref, w_ref, ba_ref, bx_ref, lam_ref,
                   o_ref, hn_ref, x_ref):
    tw = o_ref.shape[1]
    xy = jnp.dot(h_ref[...], wxy_ref[0], preferred_element_type=F32)
    x = xy[:, :tw]
    x_ref[...] = x
    taps = (b0_ref[...], b1_ref[...], b2_ref[...], x)
    xc = cb_ref[...]
    for j in range(CONV_W):
        xc = xc + taps[j] * cw_ref[j:j + 1, :]
    a, u = _lru_gates(xc, w_ref, ba_ref, bx_ref, lam_ref)
    h = u + a * h0_ref[...]
    hn_ref[...] = h
    o_ref[...] = (jax.nn.gelu(xy[:, tw:]) * h).astype(o_ref.dtype)


def _lru_step(h, w_xy, conv_buf, h0, conv_w, conv_b, w_bd, b_a, b_x, lam):
    batch = h.shape[0]
    tile = lambda off: _lru_specs(batch, lambda w: (0, off + w))
    vec = lambda rows: _lru_specs(rows, lambda w: (0, w))
    mat = lambda a: pl.BlockSpec((1,) + a.shape[1:], lambda w: (w, 0, 0))
    f32_out = jax.ShapeDtypeStruct((batch, LRU_WIDTH), F32)
    return pl.pallas_call(
        _lru_step_body,
        grid=(LRU_NTILES,),
        in_specs=[
            pl.BlockSpec(h.shape, lambda w: (0, 0)), mat(w_xy),
            tile(0), tile(LRU_NTILES), tile(2 * LRU_NTILES),
            tile(0),
            vec(CONV_W), vec(1), mat(w_bd), vec(1), vec(1), vec(1),
        ],
        out_specs=[tile(0), tile(0), tile(0)],
        out_shape=[jax.ShapeDtypeStruct((batch, LRU_WIDTH), BF16), f32_out, f32_out],
        compiler_params=_params("parallel"),
        name="lru_step",
    )(h, w_xy, conv_buf, conv_buf, conv_buf, h0, conv_w, conv_b.reshape(1, -1), w_bd,
      b_a.reshape(1, -1), b_x.reshape(1, -1), lam.reshape(1, -1))


def _merge_body(ret_ref, lru_ref, h_ref, x_ref, wr_ref, wl_ref, wg_ref, wo_ref, bg_ref, o_ref):
    h = h_ref[...]
    gate_a = jax.nn.sigmoid(jnp.dot(h, wg_ref[:, :D_MODEL], preferred_element_type=F32) + bg_ref[0:1, :])
    merged = gate_a * jnp.dot(ret_ref[...], wr_ref[...], preferred_element_type=F32)
    gate_b = jax.nn.sigmoid(jnp.dot(h, wg_ref[:, D_MODEL:], preferred_element_type=F32) + bg_ref[1:2, :])
    merged = merged + gate_b * jnp.dot(lru_ref[...], wl_ref[...], preferred_element_type=F32)
    o_ref[...] = x_ref[...] + jnp.dot(merged.astype(BF16), wo_ref[...], preferred_element_type=F32)


def _merge(ret_g, lru_g, h, x, w_ret, w_lru, w_gab, w_o, b_gate, tm):
    t = x.shape[0]
    rows = lambda n: pl.BlockSpec((tm, n), lambda i: (i, 0))
    full = lambda a: pl.BlockSpec(a.shape, lambda i: (0, 0))
    return pl.pallas_call(
        _merge_body,
        grid=(t // tm,),
        in_specs=[rows(RET_V), rows(LRU_WIDTH), rows(D_MODEL), rows(D_MODEL),
                  full(w_ret), full(w_lru), full(w_gab), full(w_o), full(b_gate)],
        out_specs=rows(D_MODEL),
        out_shape=jax.ShapeDtypeStruct((t, D_MODEL), F32),
        compiler_params=_params("parallel"),
        name="merge_out",
    )(ret_g, lru_g, h, x, w_ret, w_lru, w_gab, w_o, b_gate)


def _ffn_body(x_ref, g_ref, wgu_ref, wd_ref, o_ref):
    x = x_ref[...]
    h = (x * lax.rsqrt(jnp.mean(x * x, axis=-1, keepdims=True) + EPS) * g_ref[...]).astype(BF16)
    acc = jnp.zeros(x.shape, F32)
    for c in range(0, D_FF, FFN_CHUNK):
        gate = jnp.dot(h, wgu_ref[:, c:c + FFN_CHUNK], preferred_element_type=F32)
        up = jnp.dot(h, wgu_ref[:, D_FF + c:D_FF + c + FFN_CHUNK], preferred_element_type=F32)
        act = (jax.nn.silu(gate) * up).astype(BF16)
        acc = acc + jnp.dot(act, wd_ref[c:c + FFN_CHUNK, :], preferred_element_type=F32)
    o_ref[...] = x + acc


def _ffn(x, gain, w_gate_up, w_down, tm):
    t = x.shape[0]
    rows = pl.BlockSpec((tm, D_MODEL), lambda i: (i, 0))
    full = lambda a: pl.BlockSpec(a.shape, lambda i: (0, 0))
    return pl.pallas_call(
        _ffn_body,
        grid=(t // tm,),
        in_specs=[rows, pl.BlockSpec((1, D_MODEL), lambda i: (0, 0)), full(w_gate_up), full(w_down)],
        out_specs=rows,
        out_shape=jax.ShapeDtypeStruct((t, D_MODEL), F32),
        compiler_params=_params("parallel"),
        name="swiglu",
    )(x, gain.reshape(1, D_MODEL), w_gate_up, w_down)


def _rope_tables(pos):
    freqs = ROPE_THETA ** (-jnp.arange(ROPE_HALF, dtype=F32) / ROPE_HALF)
    ang = pos[:, None] * freqs[None, :]
    return jnp.cos(ang), jnp.sin(ang)


def _block_diag_gates(w_a, w_x):
    per_tile = LRU_TILE // LRU_BLOCK
    eye = jnp.eye(per_tile, dtype=w_a.dtype)

    def dense(w):
        w = w.reshape(LRU_NTILES, per_tile, LRU_BLOCK, LRU_BLOCK)
        return jnp.einsum("tncd,nm->tncmd", w, eye).reshape(LRU_NTILES, LRU_TILE, LRU_TILE)

    return jnp.concatenate([dense(w_a), dense(w_x)], axis=-1).astype(BF16)


def _layer_weights(l, w_in, norm_mix, gn_gain, w_ret_out, conv_w, conv_b, lru_wa, lru_ba, lru_wx, lru_bx,
                   lru_lambda, w_lru_out, b_gate, w_o, norm_ffn, w_gate_up, w_down):
    w = w_in[l].astype(BF16)
    qkv_end = 2 * RET_QK + RET_V
    gret_end = qkv_end + RET_V
    xy_end = gret_end + 2 * LRU_WIDTH
    w_xy = w[:, gret_end:xy_end].reshape(D_MODEL, 2, LRU_NTILES, LRU_TILE).transpose(2, 0, 1, 3)
    w_xy = w_xy.reshape(LRU_NTILES, D_MODEL, 2 * LRU_TILE)
    return dict(
        norm_mix=norm_mix[l], w_qkv=w[:, :qkv_end], w_gret=w[:, qkv_end:gret_end], w_qkvg=w[:, :gret_end],
        w_xy=w_xy, w_gab=w[:, xy_end:], gn_gain=gn_gain[l], w_ret_out=w_ret_out[l].astype(BF16),
        conv_w=conv_w[l], conv_b=conv_b[l], w_bd=_block_diag_gates(lru_wa[l], lru_wx[l]),
        lru_ba=lru_ba[l], lru_bx=lru_bx[l], lru_lambda=lru_lambda[l],
        w_lru_out=w_lru_out[l].astype(BF16), b_gate=b_gate[l], w_o=w_o[l].astype(BF16),
        norm_ffn=norm_ffn[l], w_gate_up=w_gate_up[l].astype(BF16), w_down=w_down[l].astype(BF16))


def _retention_front(h, p, cos, sin, tm):
    qkv = _qkv_proj(h, p["w_qkv"], cos, sin, tm)
    g_ret = _matmul(h, p["w_gret"], F32, tm, 512, "gret_proj")
    return qkv, g_ret


def _token_parallel_back(x, h, ret_g, lru_g, p, tm):
    x = _merge(ret_g, lru_g, h, x, p["w_ret_out"], p["w_lru_out"], p["w_gab"], p["w_o"], p["b_gate"], tm)
    return _ffn(x, p["norm_ffn"], p["w_gate_up"], p["w_down"], tm)


def _prompt_layer(x, p, cos, sin, batch, seq, depth, layer, states_ret):
    h = _rmsnorm(x, p["norm_mix"], BF16, 1024)
    ret_g, s_new = _retention_prompt(h, p["w_qkvg"], cos, sin, p["gn_gain"], batch, seq, depth, layer, states_ret)
    lru_g, h_last, x_tail = _lru_prompt(h.reshape(batch, seq, D_MODEL), p["w_xy"], p["conv_w"], p["conv_b"],
                                        p["w_bd"], p["lru_ba"], p["lru_bx"], p["lru_lambda"])
    new_buf = x_tail.reshape(CONV_W - 1, batch, LRU_WIDTH).transpose(1, 0, 2)
    x = _token_parallel_back(x, h, ret_g, lru_g.reshape(batch * seq, LRU_WIDTH), p, 512)
    return x, s_new, h_last, new_buf


def _sample_layer(x, p, cos, sin, layer, states_ret, new_states_ret, h_lru, conv_buf):
    batch = x.shape[0]
    tm = batch
    h = _rmsnorm(x, p["norm_mix"], BF16, tm)
    qkv, g_ret = _retention_front(h, p, cos, sin, tm)
    ret_g, new_states_ret = _retention_step(qkv, g_ret, p["gn_gain"], states_ret, layer, new_states_ret)
    ret_g = ret_g.astype(BF16)
    buf2d = conv_buf.reshape(batch, (CONV_W - 1) * LRU_WIDTH)
    lru_g, h_new, x_lru = _lru_step(h, p["w_xy"], buf2d, h_lru, p["conv_w"], p["conv_b"], p["w_bd"], p["lru_ba"],
                                    p["lru_bx"], p["lru_lambda"])
    new_buf = jnp.concatenate([conv_buf[:, 1:], x_lru[:, None, :]], axis=1)
    x = _token_parallel_back(x, h, ret_g, lru_g, p, tm)
    return x, new_states_ret, h_new, new_buf


def kernel(x_prompt, x_sample, state_ret, state_lru, state_conv, w_in, norm_mix, gn_gain, w_ret_out, conv_w,
           conv_b, lru_wa, lru_ba, lru_wx, lru_bx, lru_lambda, w_lru_out, b_gate, w_o, norm_ffn, w_gate_up,
           w_down, norm_final):
    bp, lp, _ = x_prompt.shape
    bs, ls, _ = x_sample.shape
    assert ls == 1 and lp % RET_ROWS == 0 and lp % LRU_STEPS == 0 and bs % STEP_SEQS == 0
    depth = w_in.shape[0]
    layers = [_layer_weights(l, w_in, norm_mix, gn_gain, w_ret_out, conv_w, conv_b, lru_wa, lru_ba, lru_wx, lru_bx,
                             lru_lambda, w_lru_out, b_gate, w_o, norm_ffn, w_gate_up, w_down)
              for l in range(depth)]

    cos_p, sin_p = _rope_tables(jnp.arange(lp, dtype=F32))
    cos_s, sin_s = _rope_tables(PAST_LEN + jnp.arange(ls, dtype=F32))
    cos_s = jnp.broadcast_to(cos_s, (bs, ROPE_HALF))
    sin_s = jnp.broadcast_to(sin_s, (bs, ROPE_HALF))

    xp = x_prompt.reshape(bp * lp, D_MODEL)
    xs = x_sample.reshape(bs * ls, D_MODEL)
    lru_p, conv_p, lru_s, conv_s = [], [], [], []
    ret_p = ret_s = None
    for l, p in enumerate(layers):
        xp, ret_p, h, c = _prompt_layer(xp, p, cos_p, sin_p, bp, lp, depth, l, ret_p)
        lru_p.append(h), conv_p.append(c)
        xs, ret_s, h, c = _sample_layer(xs, p, cos_s, sin_s, l, state_ret, ret_s, state_lru[l], state_conv[l])
        lru_s.append(h), conv_s.append(c)

    y_prompt = _rmsnorm(xp, norm_final, F32, 1024).reshape(bp, lp, D_MODEL)
    y_sample = _rmsnorm(xs, norm_final, F32, bs).reshape(bs, ls, D_MODEL)
    stack = lambda parts: jnp.stack(parts, 0)
    return (y_prompt, y_sample, ret_p, stack(lru_p), stack(conv_p), ret_s, stack(lru_s), stack(conv_s))
```

```python
import jax
import jax.numpy as jnp
import numpy as np
from jax import lax
from jax.experimental import pallas as pl
from jax.experimental.pallas import tpu as pltpu

F32 = jnp.float32
BF16 = jnp.bfloat16

D_MODEL = 1024
PAST_LEN = 16384
RET_HEADS = 4
RET_DK = 256
RET_DV = 512
RET_QK = RET_HEADS * RET_DK
RET_V = RET_HEADS * RET_DV
RET_CHUNK = 128
ROPE_THETA = 10000.0
ROPE_HALF = RET_DK // 2
LRU_WIDTH = 1280
LRU_BLOCKS = 16
LRU_BLOCK = LRU_WIDTH // LRU_BLOCKS
LRU_C = 8.0
CONV_W = 4
D_FF = 2816
EPS = 1e-6

COL_GATE = 2 * RET_QK + RET_V
COL_XY = COL_GATE + RET_V
COL_GAB = COL_XY + 2 * LRU_WIDTH

V7X_VMEM_BYTES = 64 * 1024 * 1024
VMEM_LIMIT_BYTES = V7X_VMEM_BYTES * 7 // 8
SUBLANES = 8
LANES = 128

LRU_TILE = int(np.lcm(LRU_BLOCK, LANES))
LRU_NTILES = LRU_WIDTH // LRU_TILE
LRU_ROWS = 256
LRU_STEPS = 256
LRU_PITCH = LRU_STEPS + SUBLANES
RET_ROWS = 512
STEP_SEQS = 4
FFN_CHUNK = 256
QKV_TILE = 512


def _params(*semantics):
    return pltpu.CompilerParams(dimension_semantics=semantics, vmem_limit_bytes=VMEM_LIMIT_BYTES)


def _layer_block(shape, layer, col_block=0):
    if callable(col_block):
        return pl.BlockSpec((1,) + shape[1:], lambda *g: (layer, 0, col_block(*g)))
    return pl.BlockSpec((1,) + shape[1:], lambda *g: (layer, 0, col_block))


def _rms(x, gain):
    return x * lax.rsqrt(jnp.mean(x * x, axis=-1, keepdims=True) + EPS) * gain


def _rmsnorm_body(x_ref, g_ref, o_ref):
    o_ref[...] = _rms(x_ref[...], g_ref[...]).astype(o_ref.dtype)


def _rmsnorm(x, gain, out_dtype, tm):
    t, d = x.shape
    return pl.pallas_call(
        _rmsnorm_body,
        grid=(t // tm,),
        in_specs=[pl.BlockSpec((tm, d), lambda i: (i, 0)), pl.BlockSpec((1, d), lambda i: (0, 0))],
        out_specs=pl.BlockSpec((tm, d), lambda i: (i, 0)),
        out_shape=jax.ShapeDtypeStruct((t, d), out_dtype),
        compiler_params=_params("parallel"),
        name="rmsnorm",
    )(x, gain.reshape(1, d))


def _rope_bf16(x, cos, sin, scale):
    x1 = x[:, :ROPE_HALF]
    x2 = x[:, ROPE_HALF:]
    return jnp.concatenate([(x1 * cos - x2 * sin) * scale, (x1 * sin + x2 * cos) * scale], axis=1).astype(BF16)


def _qkv_body(h_ref, w_ref, cos_ref, sin_ref, o_ref):
    acc = jnp.dot(h_ref[...], w_ref[0], preferred_element_type=F32)
    j = pl.program_id(1)
    n_q = RET_QK // QKV_TILE

    @pl.when(j < 2 * n_q)
    def _():
        scale = jnp.where(j < n_q, 1.0, RET_DK ** -0.5).astype(F32)
        for hd in range(QKV_TILE // RET_DK):
            cols = slice(hd * RET_DK, (hd + 1) * RET_DK)
            o_ref[:, cols] = _rope_bf16(acc[:, cols], cos_ref[...], sin_ref[...], scale)

    @pl.when(j >= 2 * n_q)
    def _():
        o_ref[...] = acc.astype(o_ref.dtype)


def _qkv_proj(h, w_in, layer, cos, sin):
    t, d = h.shape
    n = COL_GATE
    return pl.pallas_call(
        _qkv_body,
        grid=(1, n // QKV_TILE),
        in_specs=[
            pl.BlockSpec((t, d), lambda i, j: (0, 0)),
            _layer_block((1, d, QKV_TILE), layer, lambda i, j: j),
            pl.BlockSpec((t, ROPE_HALF), lambda i, j: (0, 0)),
            pl.BlockSpec((t, ROPE_HALF), lambda i, j: (0, 0)),
        ],
        out_specs=pl.BlockSpec((t, QKV_TILE), lambda i, j: (0, j)),
        out_shape=jax.ShapeDtypeStruct((t, n), BF16),
        compiler_params=_params("parallel", "arbitrary"),
        name="qkv_proj",
    )(h, w_in, cos, sin)


def _matmul_body(h_ref, w_ref, o_ref):
    o_ref[...] = jnp.dot(h_ref[...], w_ref[0], preferred_element_type=F32).astype(o_ref.dtype)


def _gate_proj(h, w_in, layer):
    t, d = h.shape
    tn = 512
    first = COL_GATE // tn
    return pl.pallas_call(
        _matmul_body,
        grid=(RET_V // tn,),
        in_specs=[pl.BlockSpec((t, d), lambda j: (0, 0)), _layer_block((1, d, tn), layer, lambda j: first + j)],
        out_specs=pl.BlockSpec((t, tn), lambda j: (0, j)),
        out_shape=jax.ShapeDtypeStruct((t, RET_V), F32),
        compiler_params=_params("arbitrary"),
        name="gret_proj",
    )(h, w_in)


def _group_norm_gate(o, gate, gain):
    mu = jnp.mean(o, axis=-1, keepdims=True)
    var = jnp.mean(jnp.square(o - mu), axis=-1, keepdims=True)
    on = (o - mu) * lax.rsqrt(var + EPS) * gain
    return jax.nn.silu(gate) * on


def _retention_body(h_ref, w_ref, cos_ref, sin_ref, gain_ref, decay_ref, qdec_ref, kdec_ref, cdec_ref,
                    *rest):
    o_ref, s_out_ref, s_ref = rest[-3:]
    blk = pl.program_id(1)

    @pl.when(blk == 0)
    def _():
        s_ref[...] = jnp.zeros_like(s_ref)

    h = h_ref[...]
    cos = cos_ref[...]
    sin = sin_ref[...]
    proj = lambda lo, n: jnp.dot(h, w_ref[0, :, lo:lo + n], preferred_element_type=F32)
    for hd in range(RET_HEADS):
        v_cols = slice(hd * RET_DV, (hd + 1) * RET_DV)
        q = _rope_bf16(proj(hd * RET_DK, RET_DK), cos, sin, 1.0)
        k = _rope_bf16(proj(RET_QK + hd * RET_DK, RET_DK), cos, sin, RET_DK ** -0.5)
        v = proj(2 * RET_QK + hd * RET_DV, RET_DV).astype(BF16)
        gate = proj(COL_GATE + hd * RET_DV, RET_DV)
        s = s_ref[hd]
        for r0 in range(0, h.shape[0], RET_CHUNK):
            rows = slice(r0, r0 + RET_CHUNK)
            qc, kc, vc = q[rows], k[rows], v[rows]
            scores = lax.dot_general(qc, kc, (((1,), (1,)), ((), ())), preferred_element_type=F32) * decay_ref[hd]
            inner = jnp.dot(scores.astype(BF16), vc, preferred_element_type=F32)
            cross = jnp.dot(qc, s.astype(BF16), preferred_element_type=F32) * qdec_ref[hd]
            kd = (kc.astype(F32) * kdec_ref[hd]).astype(BF16)
            s = s * cdec_ref[hd] + lax.dot_general(kd, vc, (((0,), (0,)), ((), ())), preferred_element_type=F32)
            o_ref[rows, v_cols] = _group_norm_gate(inner + cross, gate[rows], gain_ref[:, v_cols]).astype(o_ref.dtype)
        s_ref[hd] = s

    @pl.when(blk == pl.num_programs(1) - 1)
    def _():
        s_out_ref[0, 0] = s_ref[...]


def _retention_tables(chunk):
    log_g = jnp.log(1.0 - 2.0 ** (-5.0 - jnp.arange(RET_HEADS, dtype=F32)))
    idx = jnp.arange(chunk, dtype=F32)
    diff = idx[:, None] - idx[None, :]
    decay = jnp.where(diff >= 0, jnp.exp(log_g[:, None, None] * jnp.maximum(diff, 0.0)), 0.0)
    q_dec = jnp.exp(log_g[:, None] * (idx[None, :] + 1.0))
    k_dec = jnp.exp(log_g[:, None] * (chunk - 1.0 - idx[None, :]))
    chunk_dec = jnp.exp(log_g * chunk)
    return decay, q_dec[:, :, None], k_dec[:, :, None], chunk_dec[:, None, None]


def _retention_prompt(h, w_in, cos, sin, gain, batch, seq, layer, prev_states):
    t, d = h.shape
    decay, q_dec, k_dec, chunk_dec = _retention_tables(RET_CHUNK)
    nblk = seq // RET_ROWS
    full = lambda a: pl.BlockSpec(a.shape, lambda b, j: (0,) * a.ndim)
    pos = pl.BlockSpec((RET_ROWS, ROPE_HALF), lambda b, j: (j, 0))
    state_shape = (w_in.shape[0], batch, RET_HEADS, RET_DK, RET_DV)
    operands = [h, w_in, cos, sin, gain.reshape(1, RET_V), decay, q_dec, k_dec, chunk_dec]
    in_specs = [pl.BlockSpec((RET_ROWS, d), lambda b, j: (b * nblk + j, 0)),
                _layer_block((1, d, COL_XY), layer), pos, pos,
                full(operands[4]), full(decay), full(q_dec), full(k_dec), full(chunk_dec)]
    aliases = {}
    if prev_states is not None:
        aliases = {len(operands): 1}
        operands.append(prev_states)
        in_specs.append(pl.BlockSpec(memory_space=pl.ANY))
    return pl.pallas_call(
        _retention_body,
        grid=(batch, nblk),
        in_specs=in_specs,
        out_specs=[
            pl.BlockSpec((RET_ROWS, RET_V), lambda b, j: (b * nblk + j, 0)),
            pl.BlockSpec((1, 1) + state_shape[2:], lambda b, j: (layer, b, 0, 0, 0)),
        ],
        out_shape=[jax.ShapeDtypeStruct((t, RET_V), BF16), jax.ShapeDtypeStruct(state_shape, F32)],
        scratch_shapes=[pltpu.VMEM(state_shape[2:], F32)],
        input_output_aliases=aliases,
        compiler_params=_params("parallel", "arbitrary"),
        name="retention_prompt",
    )(*operands)


def _retention_step_body(qt_ref, kt_ref, q_ref, k_ref, v_ref, g_ref, gain_ref, dec_ref, s_ref, *rest):
    o_ref, s_out_ref = rest[-2:]
    n_seq = qt_ref.shape[1]
    seq_ids = lax.broadcasted_iota(jnp.int32, (n_seq, RET_DV), 0)
    for n in range(STEP_SEQS):
        seq = pl.program_id(0) * STEP_SEQS + n
        row = pl.ds(seq, 1)
        one_hot = (seq_ids == seq).astype(BF16)
        for hd in range(RET_HEADS):
            qk_cols = slice(hd * RET_DK, (hd + 1) * RET_DK)
            cols = slice(hd * RET_DV, (hd + 1) * RET_DV)
            q_spread = jnp.dot(qt_ref[qk_cols, :], one_hot, preferred_element_type=F32)
            k_spread = jnp.dot(kt_ref[qk_cols, :], one_hot, preferred_element_type=F32)
            s = s_ref[0, n, hd]
            v = v_ref[row, cols]
            decay = dec_ref[hd:hd + 1, 0:1]
            q_dec = dec_ref[hd:hd + 1, 1:2]
            k_dec = dec_ref[hd:hd + 1, 2:3]
            chunk_dec = dec_ref[hd:hd + 1, 3:4]
            scores = jnp.sum(q_ref[row, qk_cols] * k_ref[row, qk_cols], axis=1, keepdims=True) * decay
            cross = jnp.sum(q_spread * s, axis=0, keepdims=True) * q_dec
            s_out_ref[0, n, hd] = s * chunk_dec + k_spread * (k_dec * v)
            o = scores * v + cross
            o_ref[row, cols] = _group_norm_gate(o, g_ref[row, cols], gain_ref[:, cols])


def _retention_step(qkv, g_ret, gain, states, layer, prev_states):
    batch = qkv.shape[0]
    decay, q_dec, k_dec, chunk_dec = _retention_tables(1)
    dec = jnp.concatenate([decay[:, :, 0], q_dec[:, :, 0], k_dec[:, :, 0], chunk_dec[:, :, 0]], axis=1)
    q = qkv[:, :RET_QK]
    k = qkv[:, RET_QK:2 * RET_QK]
    full = lambda a: pl.BlockSpec(a.shape, lambda i: (0, 0))
    st = pl.BlockSpec((1, STEP_SEQS) + states.shape[2:], lambda i: (layer, i, 0, 0, 0))
    operands = [q.T, k.T, q.astype(F32), k.astype(F32), qkv[:, 2 * RET_QK:].astype(F32), g_ret,
                gain.reshape(1, RET_V), dec]
    in_specs = [full(a) for a in operands] + [st]
    operands.append(states)
    aliases = {}
    if prev_states is not None:
        aliases = {len(operands): 1}
        operands.append(prev_states)
        in_specs.append(pl.BlockSpec(memory_space=pl.ANY))
    return pl.pallas_call(
        _retention_step_body,
        grid=(batch // STEP_SEQS,),
        in_specs=in_specs,
        out_specs=[pl.BlockSpec((batch, RET_V), lambda i: (0, 0)), st],
        out_shape=[jax.ShapeDtypeStruct((batch, RET_V), F32), jax.ShapeDtypeStruct(states.shape, F32)],
        input_output_aliases=aliases,
        compiler_params=_params("arbitrary"),
        name="retention_step",
    )(*operands)


def _lru_gates(xc, w_ref, ba_ref, bx_ref, lam_ref):
    g = jnp.dot(xc.astype(BF16), w_ref[0, 0], preferred_element_type=F32)
    r = jax.nn.sigmoid(g[:, :LRU_TILE] + ba_ref[...])
    i = jax.nn.sigmoid(g[:, LRU_TILE:] + bx_ref[...])
    log_a = -LRU_C * r * jax.nn.softplus(-lam_ref[...])
    a = jnp.exp(log_a)
    th = jnp.tanh(log_a)
    u = jnp.sqrt(-2.0 * th / (1.0 - th)) * (i * xc)
    return a, u


def _lru_prompt_body(h_ref, wxy_ref, cw_ref, cb_ref, w_ref, ba_ref, bx_ref, lam_ref,
                     o_ref, hl_ref, xt_ref, xs_ref, hs_ref, y_ref, xpad_ref, state_ref):
    n_seq, steps, _ = h_ref.shape
    tw = o_ref.shape[2]
    slabs = tw // LANES
    tail = (CONV_W - 1) * SUBLANES
    sub_steps = LRU_ROWS // SUBLANES
    blk = pl.program_id(1)

    @pl.when(blk == 0)
    def _():
        xpad_ref[0:tail, :] = jnp.zeros((tail, tw), F32)
        state_ref[...] = jnp.zeros_like(state_ref)

    for b in range(n_seq):
        xy = jnp.dot(h_ref[b], wxy_ref[0, 0], preferred_element_type=F32)
        for s in range(slabs):
            xs_ref[s, b * LRU_PITCH:b * LRU_PITCH + steps, :] = xy[:, s * LANES:(s + 1) * LANES]
        y_ref[b] = xy[:, tw:]

    h = state_ref[...]
    for t0 in range(0, steps, sub_steps):
        for i in range(sub_steps):
            for s in range(slabs):
                xpad_ref[tail + i * SUBLANES:tail + (i + 1) * SUBLANES, s * LANES:(s + 1) * LANES] = (
                    xs_ref.at[s][pl.ds(t0 + i, n_seq, stride=LRU_PITCH), :])
        xc = cb_ref[...]
        for j in range(CONV_W):
            xc = xc + xpad_ref[j * SUBLANES:j * SUBLANES + LRU_ROWS, :] * cw_ref[j:j + 1, :]
        a, u = _lru_gates(xc, w_ref, ba_ref, bx_ref, lam_ref)
        for i in range(sub_steps):
            g = i * SUBLANES
            h = a[g:g + SUBLANES, :] * h + u[g:g + SUBLANES, :]
            for s in range(slabs):
                hs_ref.at[s][pl.ds(t0 + i, n_seq, stride=LRU_PITCH), :] = h[:, s * LANES:(s + 1) * LANES]
        xpad_ref[0:tail, :] = xpad_ref[LRU_ROWS:LRU_ROWS + tail, :]
    state_ref[...] = h

    for b in range(n_seq):
        plane = slice(b * LRU_PITCH, b * LRU_PITCH + steps)
        h_seq = jnp.concatenate([hs_ref[s, plane, :] for s in range(slabs)], axis=1)
        o_ref[b] = (jax.nn.gelu(y_ref[b]) * h_seq).astype(o_ref.dtype)

    @pl.when(blk == pl.num_programs(1) - 1)
    def _():
        hl_ref[...] = h
        xt_ref[...] = xpad_ref[0:tail, :]


def _lru_specs(rows, index):
    return pl.BlockSpec((rows, LRU_TILE), index)


def _lru_prompt(h, w_xy, layer, conv_w, conv_b, w_bd, b_a, b_x, lam):
    batch, seq, d = h.shape
    assert batch == SUBLANES
    tail = (CONV_W - 1) * batch
    vec = lambda r: _lru_specs(r, lambda w, t: (0, w))
    staging = pltpu.VMEM((LRU_TILE // LANES, batch * LRU_PITCH, LANES), F32)
    return pl.pallas_call(
        _lru_prompt_body,
        grid=(LRU_NTILES, seq // LRU_STEPS),
        in_specs=[
            pl.BlockSpec((batch, LRU_STEPS, d), lambda w, t: (0, t, 0)),
            pl.BlockSpec((1, 1) + w_xy.shape[2:], lambda w, t: (layer, w, 0, 0)),
            vec(CONV_W), vec(1),
            pl.BlockSpec((1, 1) + w_bd.shape[2:], lambda w, t: (layer, w, 0, 0)),
            vec(1), vec(1), vec(1),
        ],
        out_specs=[pl.BlockSpec((batch, LRU_STEPS, LRU_TILE), lambda w, t: (0, t, w)), vec(batch), vec(tail)],
        out_shape=[
            jax.ShapeDtypeStruct((batch, seq, LRU_WIDTH), BF16),
            jax.ShapeDtypeStruct((batch, LRU_WIDTH), F32),
            jax.ShapeDtypeStruct((tail, LRU_WIDTH), F32),
        ],
        scratch_shapes=[
            staging, staging,
            pltpu.VMEM((batch, LRU_STEPS, LRU_TILE), F32),
            pltpu.VMEM((tail + LRU_ROWS, LRU_TILE), F32),
            pltpu.VMEM((batch, LRU_TILE), F32),
        ],
        compiler_params=_params("parallel", "arbitrary"),
        name="lru_prompt",
    )(h, w_xy, conv_w, conv_b.reshape(1, -1), w_bd, b_a.reshape(1, -1), b_x.reshape(1, -1), lam.reshape(1, -1))


def _lru_step_body(h_ref, wxy_ref, b0_ref, b1_ref, b2_ref, h0_ref, cw_ref, cb_ref, w_ref, ba_ref, bx_ref,
                   lam_ref, o_ref, hn_ref, x_ref):
    tw = o_ref.shape[1]
    xy = jnp.dot(h_ref[...], wxy_ref[0, 0], preferred_element_type=F32)
    x = xy[:, :tw]
    x_ref[...] = x
    taps = (b0_ref[...], b1_ref[...], b2_ref[...], x)
    xc = cb_ref[...]
    for j in range(CONV_W):
        xc = xc + taps[j] * cw_ref[j:j + 1, :]
    a, u = _lru_gates(xc, w_ref, ba_ref, bx_ref, lam_ref)
    h = u + a * h0_ref[...]
    hn_ref[...] = h
    o_ref[...] = (jax.nn.gelu(xy[:, tw:]) * h).astype(o_ref.dtype)


def _lru_step(h, w_xy, layer, conv_buf, h0, conv_w, conv_b, w_bd, b_a, b_x, lam):
    batch, d = h.shape
    tile = lambda off: _lru_specs(batch, lambda w: (0, off + w))
    vec = lambda rows: _lru_specs(rows, lambda w: (0, w))
    f32_out = jax.ShapeDtypeStruct((batch, LRU_WIDTH), F32)
    return pl.pallas_call(
        _lru_step_body,
        grid=(LRU_NTILES,),
        in_specs=[
            pl.BlockSpec(h.shape, lambda w: (0, 0)),
            pl.BlockSpec((1, 1) + w_xy.shape[2:], lambda w: (layer, w, 0, 0)),
            tile(0), tile(LRU_NTILES), tile(2 * LRU_NTILES),
            tile(0),
            vec(CONV_W), vec(1),
            pl.BlockSpec((1, 1) + w_bd.shape[2:], lambda w: (layer, w, 0, 0)),
            vec(1), vec(1), vec(1),
        ],
        out_specs=[tile(0), tile(0), tile(0)],
        out_shape=[jax.ShapeDtypeStruct((batch, LRU_WIDTH), BF16), f32_out, f32_out],
        compiler_params=_params("parallel"),
        name="lru_step",
    )(h, w_xy, conv_buf, conv_buf, conv_buf, h0, conv_w, conv_b.reshape(1, -1), w_bd,
      b_a.reshape(1, -1), b_x.reshape(1, -1), lam.reshape(1, -1))


def _merge_body(ret_ref, lru_ref, h_ref, x_ref, wr_ref, wl_ref, wg_ref, wo_ref, bg_ref, o_ref):
    h = h_ref[...]
    gate_a = jax.nn.sigmoid(jnp.dot(h, wg_ref[0, :, :D_MODEL], preferred_element_type=F32) + bg_ref[0:1, :])
    merged = gate_a * jnp.dot(ret_ref[...], wr_ref[0], preferred_element_type=F32)
    gate_b = jax.nn.sigmoid(jnp.dot(h, wg_ref[0, :, D_MODEL:], preferred_element_type=F32) + bg_ref[1:2, :])
    merged = merged + gate_b * jnp.dot(lru_ref[...], wl_ref[0], preferred_element_type=F32)
    o_ref[...] = x_ref[...] + jnp.dot(merged.astype(BF16), wo_ref[0], preferred_element_type=F32)


def _merge(ret_g, lru_g, h, x, w_ret, w_lru, w_gab, w_o, layer, b_gate, tm):
    t = x.shape[0]
    rows = lambda n: pl.BlockSpec((tm, n), lambda i: (i, 0))
    whole = lambda w: _layer_block(w.shape, layer)
    return pl.pallas_call(
        _merge_body,
        grid=(t // tm,),
        in_specs=[rows(RET_V), rows(LRU_WIDTH), rows(D_MODEL), rows(D_MODEL),
                  whole(w_ret), whole(w_lru), whole(w_gab), whole(w_o), pl.BlockSpec(b_gate.shape, lambda i: (0, 0))],
        out_specs=rows(D_MODEL),
        out_shape=jax.ShapeDtypeStruct((t, D_MODEL), F32),
        compiler_params=_params("parallel"),
        name="merge_out",
    )(ret_g, lru_g, h, x, w_ret, w_lru, w_gab, w_o, b_gate)


def _ffn_body(x_ref, g_ref, wgu_ref, wd_ref, gn_ref, *out_refs):
    x = x_ref[...]
    h = _rms(x, g_ref[...]).astype(BF16)
    acc = jnp.zeros(x.shape, F32)
    for c in range(0, D_FF, FFN_CHUNK):
        gate = jnp.dot(h, wgu_ref[0, :, c:c + FFN_CHUNK], preferred_element_type=F32)
        up = jnp.dot(h, wgu_ref[0, :, D_FF + c:D_FF + c + FFN_CHUNK], preferred_element_type=F32)
        act = (jax.nn.silu(gate) * up).astype(BF16)
        acc = acc + jnp.dot(act, wd_ref[0, c:c + FFN_CHUNK, :], preferred_element_type=F32)
    x_new = x + acc
    if len(out_refs) == 2:
        out_refs[0][...] = x_new
    out_refs[-1][...] = _rms(x_new, gn_ref[...]).astype(out_refs[-1].dtype)


def _ffn(x, gain, w_gate_up, w_down, layer, next_gain, last, tm):
    t = x.shape[0]
    rows = pl.BlockSpec((tm, D_MODEL), lambda i: (i, 0))
    vec = pl.BlockSpec((1, D_MODEL), lambda i: (0, 0))
    if last:
        out_specs, out_shape = [rows], [jax.ShapeDtypeStruct((t, D_MODEL), F32)]
    else:
        out_specs = [rows, rows]
        out_shape = [jax.ShapeDtypeStruct((t, D_MODEL), F32), jax.ShapeDtypeStruct((t, D_MODEL), BF16)]
    return pl.pallas_call(
        _ffn_body,
        grid=(t // tm,),
        in_specs=[rows, vec, _layer_block(w_gate_up.shape, layer), _layer_block(w_down.shape, layer), vec],
        out_specs=out_specs,
        out_shape=out_shape,
        compiler_params=_params("parallel"),
        name="swiglu",
    )(x, gain.reshape(1, D_MODEL), w_gate_up, w_down, next_gain.reshape(1, D_MODEL))


def _rope_tables(pos):
    freqs = ROPE_THETA ** (-jnp.arange(ROPE_HALF, dtype=F32) / ROPE_HALF)
    ang = pos[:, None] * freqs[None, :]
    return jnp.cos(ang), jnp.sin(ang)


def _block_diag_gates(w_a, w_x):
    depth = w_a.shape[0]
    per_tile = LRU_TILE // LRU_BLOCK
    eye = jnp.eye(per_tile, dtype=w_a.dtype)

    def dense(w):
        w = w.reshape(depth, LRU_NTILES, per_tile, LRU_BLOCK, LRU_BLOCK)
        return jnp.einsum("ltncd,nm->ltncmd", w, eye).reshape(depth, LRU_NTILES, LRU_TILE, LRU_TILE)

    return jnp.concatenate([dense(w_a), dense(w_x)], axis=-1).astype(BF16)


def kernel(x_prompt, x_sample, state_ret, state_lru, state_conv, w_in, norm_mix, gn_gain, w_ret_out, conv_w,
           conv_b, lru_wa, lru_ba, lru_wx, lru_bx, lru_lambda, w_lru_out, b_gate, w_o, norm_ffn, w_gate_up,
           w_down, norm_final):
    bp, lp, _ = x_prompt.shape
    bs, ls, _ = x_sample.shape
    assert ls == 1 and lp % RET_ROWS == 0 and lp % LRU_STEPS == 0 and bs % STEP_SEQS == 0
    depth = w_in.shape[0]

    w_in = w_in.astype(BF16)
    tile_cols = lambda lo: w_in[:, :, lo:lo + LRU_TILE]
    w_xy = jnp.stack([jnp.concatenate([tile_cols(COL_XY + w * LRU_TILE), tile_cols(COL_XY + LRU_WIDTH + w * LRU_TILE)],
                                      axis=-1) for w in range(LRU_NTILES)], axis=1)
    w_gab = w_in[:, :, COL_GAB:]
    w_ret_out, w_lru_out, w_o = w_ret_out.astype(BF16), w_lru_out.astype(BF16), w_o.astype(BF16)
    w_gate_up, w_down = w_gate_up.astype(BF16), w_down.astype(BF16)
    w_bd = _block_diag_gates(lru_wa, lru_wx)

    cos_p, sin_p = _rope_tables(jnp.arange(lp, dtype=F32))
    cos_s, sin_s = _rope_tables(PAST_LEN + jnp.arange(ls, dtype=F32))
    cos_s = jnp.broadcast_to(cos_s, (bs, ROPE_HALF))
    sin_s = jnp.broadcast_to(sin_s, (bs, ROPE_HALF))

    xp = x_prompt.reshape(bp * lp, D_MODEL)
    xs = x_sample.reshape(bs * ls, D_MODEL)
    hp = _rmsnorm(xp, norm_mix[0], BF16, 1024)
    hs = _rmsnorm(xs, norm_mix[0], BF16, bs)
    lru_p, conv_p, lru_s, conv_s = [], [], [], []
    ret_p = ret_s = None
    for l in range(depth):
        last = l == depth - 1
        next_gain = norm_final if last else norm_mix[l + 1]
        lru_args = (conv_w[l], conv_b[l], w_bd, lru_ba[l], lru_bx[l], lru_lambda[l])

        ret_g, ret_p = _retention_prompt(hp, w_in, cos_p, sin_p, gn_gain[l], bp, lp, l, ret_p)
        lru_g, h_last, x_tail = _lru_prompt(hp.reshape(bp, lp, D_MODEL), w_xy, l, *lru_args)
        lru_p.append(h_last)
        conv_p.append(x_tail.reshape(CONV_W - 1, bp, LRU_WIDTH).transpose(1, 0, 2))
        xp = _merge(ret_g, lru_g.reshape(bp * lp, LRU_WIDTH), hp, xp, w_ret_out, w_lru_out, w_gab, w_o, l,
                    b_gate[l], 512)
        out = _ffn(xp, norm_ffn[l], w_gate_up, w_down, l, next_gain, last, 512)
        xp, hp = (None, out[0]) if last else out

        qkv = _qkv_proj(hs, w_in, l, cos_s, sin_s)
        g_ret = _gate_proj(hs, w_in, l)
        ret_g, ret_s = _retention_step(qkv, g_ret, gn_gain[l], state_ret, l, ret_s)
        buf2d = state_conv[l].reshape(bs, (CONV_W - 1) * LRU_WIDTH)
        lru_g, h_new, x_lru = _lru_step(hs, w_xy, l, buf2d, state_lru[l], *lru_args)
        lru_s.append(h_new)
        conv_s.append(jnp.concatenate([state_conv[l][:, 1:], x_lru[:, None, :]], axis=1))
        xs = _merge(ret_g.astype(BF16), lru_g, hs, xs, w_ret_out, w_lru_out, w_gab, w_o, l, b_gate[l], bs)
        out = _ffn(xs, norm_ffn[l], w_gate_up, w_down, l, next_gain, last, bs)
        xs, hs = (None, out[0]) if last else out

    stack = lambda parts: jnp.stack(parts, 0)
    return (hp.reshape(bp, lp, D_MODEL), hs.reshape(bs, ls, D_MODEL), ret_p, stack(lru_p), stack(conv_p),
            ret_s, stack(lru_s), stack(conv_s))
```

```python
import functools

import jax
import jax.numpy as jnp
import numpy as np
from jax import lax
from jax.experimental import pallas as pl
from jax.experimental.pallas import tpu as pltpu

F32 = jnp.float32
BF16 = jnp.bfloat16

D_MODEL = 1024
PAST_LEN = 16384
RET_HEADS = 4
RET_DK = 256
RET_DV = 512
RET_QK = RET_HEADS * RET_DK
RET_V = RET_HEADS * RET_DV
RET_CHUNK = 128
ROPE_THETA = 10000.0
ROPE_HALF = RET_DK // 2
LRU_WIDTH = 1280
LRU_BLOCKS = 16
LRU_BLOCK = LRU_WIDTH // LRU_BLOCKS
LRU_C = 8.0
CONV_W = 4
D_FF = 2816
EPS = 1e-6

COL_GATE = 2 * RET_QK + RET_V
COL_XY = COL_GATE + RET_V
COL_GAB = COL_XY + 2 * LRU_WIDTH

V7X_VMEM_BYTES = 64 * 1024 * 1024
VMEM_LIMIT_BYTES = V7X_VMEM_BYTES * 7 // 8
SUBLANES = 8
LANES = 128

LRU_TILE = int(np.lcm(LRU_BLOCK, LANES))
LRU_NTILES = LRU_WIDTH // LRU_TILE
LRU_ROWS = 256
LRU_STEPS = 256
LRU_PITCH = LRU_STEPS + SUBLANES
RET_ROWS = 512
FFN_ROWS = 256
FFN_CHUNK = 256
QKV_TILE = 512


def _params(*semantics):
    return pltpu.CompilerParams(dimension_semantics=semantics, vmem_limit_bytes=VMEM_LIMIT_BYTES)


def _layer_block(shape, layer, col_block=0):
    if callable(col_block):
        return pl.BlockSpec((1,) + shape[1:], lambda *g: (layer, 0, col_block(*g)))
    return pl.BlockSpec((1,) + shape[1:], lambda *g: (layer, 0, col_block))


def _rms(x, gain):
    return x * lax.rsqrt(jnp.mean(x * x, axis=-1, keepdims=True) + EPS) * gain


def _rmsnorm_body(x_ref, g_ref, o_ref):
    o_ref[...] = _rms(x_ref[...], g_ref[...]).astype(o_ref.dtype)


def _rmsnorm(x, gain, out_dtype, tm):
    t, d = x.shape
    return pl.pallas_call(
        _rmsnorm_body,
        grid=(t // tm,),
        in_specs=[pl.BlockSpec((tm, d), lambda i: (i, 0)), pl.BlockSpec((1, d), lambda i: (0, 0))],
        out_specs=pl.BlockSpec((tm, d), lambda i: (i, 0)),
        out_shape=jax.ShapeDtypeStruct((t, d), out_dtype),
        compiler_params=_params("parallel"),
        name="rmsnorm",
    )(x, gain.reshape(1, d))


def _rope_bf16(x, cos, sin, scale):
    x1 = x[:, :ROPE_HALF]
    x2 = x[:, ROPE_HALF:]
    return jnp.concatenate([(x1 * cos - x2 * sin) * scale, (x1 * sin + x2 * cos) * scale], axis=1).astype(BF16)


def _qkv_body(h_ref, w_ref, cos_ref, sin_ref, o_ref):
    acc = jnp.dot(h_ref[...], w_ref[0], preferred_element_type=F32)
    j = pl.program_id(1)
    n_q = RET_QK // QKV_TILE

    @pl.when(j < 2 * n_q)
    def _():
        scale = jnp.where(j < n_q, 1.0, RET_DK ** -0.5).astype(F32)
        for hd in range(QKV_TILE // RET_DK):
            cols = slice(hd * RET_DK, (hd + 1) * RET_DK)
            o_ref[:, cols] = _rope_bf16(acc[:, cols], cos_ref[...], sin_ref[...], scale)

    @pl.when(j >= 2 * n_q)
    def _():
        o_ref[...] = acc.astype(o_ref.dtype)


def _qkv_proj(h, w_in, layer, cos, sin):
    t, d = h.shape
    n = COL_GATE
    return pl.pallas_call(
        _qkv_body,
        grid=(1, n // QKV_TILE),
        in_specs=[
            pl.BlockSpec((t, d), lambda i, j: (0, 0)),
            _layer_block((1, d, QKV_TILE), layer, lambda i, j: j),
            pl.BlockSpec((t, ROPE_HALF), lambda i, j: (0, 0)),
            pl.BlockSpec((t, ROPE_HALF), lambda i, j: (0, 0)),
        ],
        out_specs=pl.BlockSpec((t, QKV_TILE), lambda i, j: (0, j)),
        out_shape=jax.ShapeDtypeStruct((t, n), BF16),
        compiler_params=_params("parallel", "arbitrary"),
        name="qkv_proj",
    )(h, w_in, cos, sin)


def _matmul_body(h_ref, w_ref, o_ref):
    o_ref[...] = jnp.dot(h_ref[...], w_ref[0], preferred_element_type=F32).astype(o_ref.dtype)


def _gate_proj(h, w_in, layer):
    t, d = h.shape
    tn = 512
    first = COL_GATE // tn
    return pl.pallas_call(
        _matmul_body,
        grid=(RET_V // tn,),
        in_specs=[pl.BlockSpec((t, d), lambda j: (0, 0)), _layer_block((1, d, tn), layer, lambda j: first + j)],
        out_specs=pl.BlockSpec((t, tn), lambda j: (0, j)),
        out_shape=jax.ShapeDtypeStruct((t, RET_V), F32),
        compiler_params=_params("arbitrary"),
        name="gret_proj",
    )(h, w_in)


def _group_norm_gate(o, gate, gain):
    mu = jnp.mean(o, axis=-1, keepdims=True)
    var = jnp.mean(jnp.square(o - mu), axis=-1, keepdims=True)
    on = (o - mu) * lax.rsqrt(var + EPS) * gain
    return jax.nn.silu(gate) * on


def _retention_body(h_ref, w_ref, cos_ref, sin_ref, gain_ref, decay_ref, qdec_ref, kdec_ref, cdec_ref,
                    *rest):
    o_ref, s_out_ref, s_ref = rest[-3:]
    blk = pl.program_id(1)

    @pl.when(blk == 0)
    def _():
        s_ref[...] = jnp.zeros_like(s_ref)

    h = h_ref[...]
    cos = cos_ref[...]
    sin = sin_ref[...]
    proj = lambda lo, n: jnp.dot(h, w_ref[0, :, lo:lo + n], preferred_element_type=F32)
    for hd in range(RET_HEADS):
        v_cols = slice(hd * RET_DV, (hd + 1) * RET_DV)
        q = _rope_bf16(proj(hd * RET_DK, RET_DK), cos, sin, 1.0)
        k = _rope_bf16(proj(RET_QK + hd * RET_DK, RET_DK), cos, sin, RET_DK ** -0.5)
        v = proj(2 * RET_QK + hd * RET_DV, RET_DV).astype(BF16)
        gate = proj(COL_GATE + hd * RET_DV, RET_DV)
        s = s_ref[hd]
        for r0 in range(0, h.shape[0], RET_CHUNK):
            rows = slice(r0, r0 + RET_CHUNK)
            qc, kc, vc = q[rows], k[rows], v[rows]
            scores = lax.dot_general(qc, kc, (((1,), (1,)), ((), ())), preferred_element_type=F32) * decay_ref[hd]
            inner = jnp.dot(scores.astype(BF16), vc, preferred_element_type=F32)
            cross = jnp.dot(qc, s.astype(BF16), preferred_element_type=F32) * qdec_ref[hd]
            kd = (kc.astype(F32) * kdec_ref[hd]).astype(BF16)
            s = s * cdec_ref[hd] + lax.dot_general(kd, vc, (((0,), (0,)), ((), ())), preferred_element_type=F32)
            o_ref[rows, v_cols] = _group_norm_gate(inner + cross, gate[rows], gain_ref[:, v_cols]).astype(o_ref.dtype)
        s_ref[hd] = s

    @pl.when(blk == pl.num_programs(1) - 1)
    def _():
        s_out_ref[0, 0] = s_ref[...]


def _retention_tables(chunk):
    log_g = jnp.log(1.0 - 2.0 ** (-5.0 - jnp.arange(RET_HEADS, dtype=F32)))
    idx = jnp.arange(chunk, dtype=F32)
    diff = idx[:, None] - idx[None, :]
    decay = jnp.where(diff >= 0, jnp.exp(log_g[:, None, None] * jnp.maximum(diff, 0.0)), 0.0)
    q_dec = jnp.exp(log_g[:, None] * (idx[None, :] + 1.0))
    k_dec = jnp.exp(log_g[:, None] * (chunk - 1.0 - idx[None, :]))
    chunk_dec = jnp.exp(log_g * chunk)
    return decay, q_dec[:, :, None], k_dec[:, :, None], chunk_dec[:, None, None]


def _retention_prompt(h, w_in, cos, sin, gain, batch, seq, layer, prev_states):
    t, d = h.shape
    decay, q_dec, k_dec, chunk_dec = _retention_tables(RET_CHUNK)
    nblk = seq // RET_ROWS
    full = lambda a: pl.BlockSpec(a.shape, lambda b, j: (0,) * a.ndim)
    pos = pl.BlockSpec((RET_ROWS, ROPE_HALF), lambda b, j: (j, 0))
    state_shape = (w_in.shape[0], batch, RET_HEADS, RET_DK, RET_DV)
    operands = [h, w_in, cos, sin, gain.reshape(1, RET_V), decay, q_dec, k_dec, chunk_dec]
    in_specs = [pl.BlockSpec((RET_ROWS, d), lambda b, j: (b * nblk + j, 0)),
                _layer_block((1, d, COL_XY), layer), pos, pos,
                full(operands[4]), full(decay), full(q_dec), full(k_dec), full(chunk_dec)]
    aliases = {}
    if prev_states is not None:
        aliases = {len(operands): 1}
        operands.append(prev_states)
        in_specs.append(pl.BlockSpec(memory_space=pl.ANY))
    return pl.pallas_call(
        _retention_body,
        grid=(batch, nblk),
        in_specs=in_specs,
        out_specs=[
            pl.BlockSpec((RET_ROWS, RET_V), lambda b, j: (b * nblk + j, 0)),
            pl.BlockSpec((1, 1) + state_shape[2:], lambda b, j: (layer, b, 0, 0, 0)),
        ],
        out_shape=[jax.ShapeDtypeStruct((t, RET_V), BF16), jax.ShapeDtypeStruct(state_shape, F32)],
        scratch_shapes=[pltpu.VMEM(state_shape[2:], F32)],
        input_output_aliases=aliases,
        compiler_params=_params("parallel", "arbitrary"),
        name="retention_prompt",
    )(*operands)


def _retention_step_parts(qt_ref, kt_ref, q_ref, k_ref, v_ref, g_ref, gain_ref, dec_ref, s_ref, *rest):
    o_ref, s_out_ref = rest[-2:]
    n_seq = qt_ref.shape[1]
    step_seqs = s_ref.shape[1]

    def part(n, hd):
        seq = pl.program_id(0) * step_seqs + n
        row = pl.ds(seq, 1)
        one_hot = (lax.broadcasted_iota(jnp.int32, (n_seq, RET_DV), 0) == seq).astype(BF16)
        qk_cols = slice(hd * RET_DK, (hd + 1) * RET_DK)
        cols = slice(hd * RET_DV, (hd + 1) * RET_DV)
        q_spread = jnp.dot(qt_ref[qk_cols, :], one_hot, preferred_element_type=F32)
        k_spread = jnp.dot(kt_ref[qk_cols, :], one_hot, preferred_element_type=F32)
        s = s_ref[0, n, hd]
        v = v_ref[row, cols]
        decay = dec_ref[hd:hd + 1, 0:1]
        q_dec = dec_ref[hd:hd + 1, 1:2]
        k_dec = dec_ref[hd:hd + 1, 2:3]
        chunk_dec = dec_ref[hd:hd + 1, 3:4]
        scores = jnp.sum(q_ref[row, qk_cols] * k_ref[row, qk_cols], axis=1, keepdims=True) * decay
        cross = jnp.sum(q_spread * s, axis=0, keepdims=True) * q_dec
        s_out_ref[0, n, hd] = s * chunk_dec + k_spread * (k_dec * v)
        o = scores * v + cross
        o_ref[row, cols] = _group_norm_gate(o, g_ref[row, cols], gain_ref[:, cols])

    return [functools.partial(part, n, hd) for n in range(step_seqs) for hd in range(RET_HEADS)]


def _retention_step_call(qkv, g_ret, gain, states, layer, prev_states, steps):
    batch = qkv.shape[0]
    decay, q_dec, k_dec, chunk_dec = _retention_tables(1)
    dec = jnp.concatenate([decay[:, :, 0], q_dec[:, :, 0], k_dec[:, :, 0], chunk_dec[:, :, 0]], axis=1)
    q = qkv[:, :RET_QK]
    k = qkv[:, RET_QK:2 * RET_QK]
    full = lambda a: pl.BlockSpec(a.shape, lambda i: (0, 0))
    st = pl.BlockSpec((1, batch // steps) + states.shape[2:], lambda i: (layer, i, 0, 0, 0))
    operands = [q.T, k.T, q.astype(F32), k.astype(F32), qkv[:, 2 * RET_QK:].astype(F32), g_ret,
                gain.reshape(1, RET_V), dec]
    in_specs = [full(a) for a in operands] + [st]
    operands.append(states)
    alias_index = None
    if prev_states is not None:
        alias_index = len(operands)
        operands.append(prev_states)
        in_specs.append(pl.BlockSpec(memory_space=pl.ANY))
    out_specs = [pl.BlockSpec((batch, RET_V), lambda i: (0, 0)), st]
    out_shape = [jax.ShapeDtypeStruct((batch, RET_V), F32), jax.ShapeDtypeStruct(states.shape, F32)]
    return operands, in_specs, out_specs, out_shape, alias_index


def _lru_gates(xc, w_ref, ba_ref, bx_ref, lam_ref):
    g = jnp.dot(xc.astype(BF16), w_ref[0, 0], preferred_element_type=F32)
    r = jax.nn.sigmoid(g[:, :LRU_TILE] + ba_ref[...])
    i = jax.nn.sigmoid(g[:, LRU_TILE:] + bx_ref[...])
    log_a = -LRU_C * r * jax.nn.softplus(-lam_ref[...])
    a = jnp.exp(log_a)
    th = jnp.tanh(log_a)
    u = jnp.sqrt(-2.0 * th / (1.0 - th)) * (i * xc)
    return a, u


def _lru_prompt_body(h_ref, wxy_ref, cw_ref, cb_ref, w_ref, ba_ref, bx_ref, lam_ref,
                     o_ref, hl_ref, xt_ref, xs_ref, hs_ref, y_ref, xpad_ref, state_ref):
    n_seq, steps, _ = h_ref.shape
    tw = o_ref.shape[2]
    slabs = tw // LANES
    tail = (CONV_W - 1) * SUBLANES
    sub_steps = LRU_ROWS // SUBLANES
    blk = pl.program_id(1)

    @pl.when(blk == 0)
    def _():
        xpad_ref[0:tail, :] = jnp.zeros((tail, tw), F32)
        state_ref[...] = jnp.zeros_like(state_ref)

    for b in range(n_seq):
        xy = jnp.dot(h_ref[b], wxy_ref[0, 0], preferred_element_type=F32)
        for s in range(slabs):
            xs_ref[s, b * LRU_PITCH:b * LRU_PITCH + steps, :] = xy[:, s * LANES:(s + 1) * LANES]
        y_ref[b] = xy[:, tw:]

    h = state_ref[...]
    for t0 in range(0, steps, sub_steps):
        for i in range(sub_steps):
            for s in range(slabs):
                xpad_ref[tail + i * SUBLANES:tail + (i + 1) * SUBLANES, s * LANES:(s + 1) * LANES] = (
                    xs_ref.at[s][pl.ds(t0 + i, n_seq, stride=LRU_PITCH), :])
        xc = cb_ref[...]
        for j in range(CONV_W):
            xc = xc + xpad_ref[j * SUBLANES:j * SUBLANES + LRU_ROWS, :] * cw_ref[j:j + 1, :]
        a, u = _lru_gates(xc, w_ref, ba_ref, bx_ref, lam_ref)
        for i in range(sub_steps):
            g = i * SUBLANES
            h = a[g:g + SUBLANES, :] * h + u[g:g + SUBLANES, :]
            for s in range(slabs):
                hs_ref.at[s][pl.ds(t0 + i, n_seq, stride=LRU_PITCH), :] = h[:, s * LANES:(s + 1) * LANES]
        xpad_ref[0:tail, :] = xpad_ref[LRU_ROWS:LRU_ROWS + tail, :]
    state_ref[...] = h

    for b in range(n_seq):
        plane = slice(b * LRU_PITCH, b * LRU_PITCH + steps)
        h_seq = jnp.concatenate([hs_ref[s, plane, :] for s in range(slabs)], axis=1)
        o_ref[b] = (jax.nn.gelu(y_ref[b]) * h_seq).astype(o_ref.dtype)

    @pl.when(blk == pl.num_programs(1) - 1)
    def _():
        hl_ref[...] = h
        xt_ref[...] = xpad_ref[0:tail, :]


def _lru_specs(rows, index):
    return pl.BlockSpec((rows, LRU_TILE), index)


def _lru_prompt(h, w_xy, layer, conv_w, conv_b, w_bd, b_a, b_x, lam):
    batch, seq, d = h.shape
    assert batch == SUBLANES
    tail = (CONV_W - 1) * batch
    vec = lambda r: _lru_specs(r, lambda w, t: (0, w))
    staging = pltpu.VMEM((LRU_TILE // LANES, batch * LRU_PITCH, LANES), F32)
    return pl.pallas_call(
        _lru_prompt_body,
        grid=(LRU_NTILES, seq // LRU_STEPS),
        in_specs=[
            pl.BlockSpec((batch, LRU_STEPS, d), lambda w, t: (0, t, 0)),
            pl.BlockSpec((1, 1) + w_xy.shape[2:], lambda w, t: (layer, w, 0, 0)),
            vec(CONV_W), vec(1),
            pl.BlockSpec((1, 1) + w_bd.shape[2:], lambda w, t: (layer, w, 0, 0)),
            vec(1), vec(1), vec(1),
        ],
        out_specs=[pl.BlockSpec((batch, LRU_STEPS, LRU_TILE), lambda w, t: (0, t, w)), vec(batch), vec(tail)],
        out_shape=[
            jax.ShapeDtypeStruct((batch, seq, LRU_WIDTH), BF16),
            jax.ShapeDtypeStruct((batch, LRU_WIDTH), F32),
            jax.ShapeDtypeStruct((tail, LRU_WIDTH), F32),
        ],
        scratch_shapes=[
            staging, staging,
            pltpu.VMEM((batch, LRU_STEPS, LRU_TILE), F32),
            pltpu.VMEM((tail + LRU_ROWS, LRU_TILE), F32),
            pltpu.VMEM((batch, LRU_TILE), F32),
        ],
        compiler_params=_params("parallel", "arbitrary"),
        name="lru_prompt",
    )(h, w_xy, conv_w, conv_b.reshape(1, -1), w_bd, b_a.reshape(1, -1), b_x.reshape(1, -1), lam.reshape(1, -1))


def _lru_step_body(h_ref, wxy_ref, b0_ref, b1_ref, b2_ref, h0_ref, cw_ref, cb_ref, w_ref, ba_ref, bx_ref,
                   lam_ref, o_ref, hn_ref, x_ref):
    tw = o_ref.shape[1]
    xy = jnp.dot(h_ref[...], wxy_ref[0, 0], preferred_element_type=F32)
    x = xy[:, :tw]
    x_ref[...] = x
    taps = (b0_ref[...], b1_ref[...], b2_ref[...], x)
    xc = cb_ref[...]
    for j in range(CONV_W):
        xc = xc + taps[j] * cw_ref[j:j + 1, :]
    a, u = _lru_gates(xc, w_ref, ba_ref, bx_ref, lam_ref)
    h = u + a * h0_ref[...]
    hn_ref[...] = h
    o_ref[...] = (jax.nn.gelu(xy[:, tw:]) * h).astype(o_ref.dtype)


def _lru_step(h, w_xy, layer, conv_buf, h0, conv_w, conv_b, w_bd, b_a, b_x, lam):
    batch, d = h.shape
    tile = lambda off: _lru_specs(batch, lambda w: (0, off + w))
    vec = lambda rows: _lru_specs(rows, lambda w: (0, w))
    f32_out = jax.ShapeDtypeStruct((batch, LRU_WIDTH), F32)
    return pl.pallas_call(
        _lru_step_body,
        grid=(LRU_NTILES,),
        in_specs=[
            pl.BlockSpec(h.shape, lambda w: (0, 0)),
            pl.BlockSpec((1, 1) + w_xy.shape[2:], lambda w: (layer, w, 0, 0)),
            tile(0), tile(LRU_NTILES), tile(2 * LRU_NTILES),
            tile(0),
            vec(CONV_W), vec(1),
            pl.BlockSpec((1, 1) + w_bd.shape[2:], lambda w: (layer, w, 0, 0)),
            vec(1), vec(1), vec(1),
        ],
        out_specs=[tile(0), tile(0), tile(0)],
        out_shape=[jax.ShapeDtypeStruct((batch, LRU_WIDTH), BF16), f32_out, f32_out],
        compiler_params=_params("parallel"),
        name="lru_step",
    )(h, w_xy, conv_buf, conv_buf, conv_buf, h0, conv_w, conv_b.reshape(1, -1), w_bd,
      b_a.reshape(1, -1), b_x.reshape(1, -1), lam.reshape(1, -1))


def _merge_body(ret_ref, lru_ref, h_ref, x_ref, wr_ref, wl_ref, wg_ref, wo_ref, bg_ref, o_ref):
    h = h_ref[...]
    gate_a = jax.nn.sigmoid(jnp.dot(h, wg_ref[0, :, :D_MODEL], preferred_element_type=F32) + bg_ref[0:1, :])
    merged = gate_a * jnp.dot(ret_ref[...], wr_ref[0], preferred_element_type=F32)
    gate_b = jax.nn.sigmoid(jnp.dot(h, wg_ref[0, :, D_MODEL:], preferred_element_type=F32) + bg_ref[1:2, :])
    merged = merged + gate_b * jnp.dot(lru_ref[...], wl_ref[0], preferred_element_type=F32)
    o_ref[...] = x_ref[...] + jnp.dot(merged.astype(BF16), wo_ref[0], preferred_element_type=F32)


def _merge(ret_g, lru_g, h, x, w_ret, w_lru, w_gab, w_o, layer, b_gate, tm):
    t = x.shape[0]
    rows = lambda n: pl.BlockSpec((tm, n), lambda i: (i, 0))
    whole = lambda w: _layer_block(w.shape, layer)
    return pl.pallas_call(
        _merge_body,
        grid=(t // tm,),
        in_specs=[rows(RET_V), rows(LRU_WIDTH), rows(D_MODEL), rows(D_MODEL),
                  whole(w_ret), whole(w_lru), whole(w_gab), whole(w_o), pl.BlockSpec(b_gate.shape, lambda i: (0, 0))],
        out_specs=rows(D_MODEL),
        out_shape=jax.ShapeDtypeStruct((t, D_MODEL), F32),
        compiler_params=_params("parallel"),
        name="merge_out",
    )(ret_g, lru_g, h, x, w_ret, w_lru, w_gab, w_o, b_gate)


def _ffn_body(x_ref, g_ref, wgu_ref, wd_ref, gn_ref, *out_refs, riders=()):
    riders = list(riders)
    x = x_ref[...]
    h = _rms(x, g_ref[...]).astype(BF16)
    acc = jnp.zeros(x.shape, F32)
    for c in range(0, D_FF, FFN_CHUNK):
        gate = jnp.dot(h, wgu_ref[0, :, c:c + FFN_CHUNK], preferred_element_type=F32)
        up = jnp.dot(h, wgu_ref[0, :, D_FF + c:D_FF + c + FFN_CHUNK], preferred_element_type=F32)
        act = (jax.nn.silu(gate) * up).astype(BF16)
        acc = acc + jnp.dot(act, wd_ref[0, c:c + FFN_CHUNK, :], preferred_element_type=F32)
        if riders:
            riders.pop(0)()
    for rider in riders:
        rider()
    x_new = x + acc
    if len(out_refs) == 2:
        out_refs[0][...] = x_new
    out_refs[-1][...] = _rms(x_new, gn_ref[...]).astype(out_refs[-1].dtype)


def _ffn_and_state_body(n_ffn_in, n_in, n_ffn_out, *refs):
    ins, outs = refs[:n_in], refs[n_in:]
    riders = _retention_step_parts(*ins[n_ffn_in:], *outs[n_ffn_out:])
    _ffn_body(*ins[:n_ffn_in], *outs[:n_ffn_out], riders=riders)


def _ffn(x, gain, w_gate_up, w_down, layer, next_gain, last, tm, rider=None):
    t = x.shape[0]
    steps = t // tm
    rows = pl.BlockSpec((tm, D_MODEL), lambda i: (i, 0))
    vec = pl.BlockSpec((1, D_MODEL), lambda i: (0, 0))
    buffering = None if rider is None else pl.Buffered(1)
    whole = lambda w: pl.BlockSpec((1,) + w.shape[1:], lambda i: (layer, 0, 0), pipeline_mode=buffering)
    operands = [x, gain.reshape(1, D_MODEL), w_gate_up, w_down, next_gain.reshape(1, D_MODEL)]
    in_specs = [rows, vec, whole(w_gate_up), whole(w_down), vec]
    if last:
        out_specs, out_shape = [rows], [jax.ShapeDtypeStruct((t, D_MODEL), F32)]
    else:
        out_specs = [rows, rows]
        out_shape = [jax.ShapeDtypeStruct((t, D_MODEL), F32), jax.ShapeDtypeStruct((t, D_MODEL), BF16)]
    body, aliases, semantics = _ffn_body, {}, "parallel"
    if rider is not None:
        r_operands, r_in_specs, r_out_specs, r_out_shape, r_alias = rider(steps)
        n_ffn_in, n_ffn_out = len(operands), len(out_specs)
        if r_alias is not None:
            aliases = {n_ffn_in + r_alias: n_ffn_out + 1}
        operands, in_specs = operands + r_operands, in_specs + r_in_specs
        out_specs, out_shape = out_specs + r_out_specs, out_shape + r_out_shape
        body = functools.partial(_ffn_and_state_body, n_ffn_in, len(operands), n_ffn_out)
        semantics = "arbitrary"
    return pl.pallas_call(
        body,
        grid=(steps,),
        in_specs=in_specs,
        out_specs=out_specs,
        out_shape=out_shape,
        input_output_aliases=aliases,
        compiler_params=_params(semantics),
        name="swiglu",
    )(*operands)


def _rope_tables(pos):
    freqs = ROPE_THETA ** (-jnp.arange(ROPE_HALF, dtype=F32) / ROPE_HALF)
    ang = pos[:, None] * freqs[None, :]
    return jnp.cos(ang), jnp.sin(ang)


def _block_diag_gates(w_a, w_x):
    depth = w_a.shape[0]
    per_tile = LRU_TILE // LRU_BLOCK
    eye = jnp.eye(per_tile, dtype=w_a.dtype)

    def dense(w):
        w = w.reshape(depth, LRU_NTILES, per_tile, LRU_BLOCK, LRU_BLOCK)
        return jnp.einsum("ltncd,nm->ltncmd", w, eye).reshape(depth, LRU_NTILES, LRU_TILE, LRU_TILE)

    return jnp.concatenate([dense(w_a), dense(w_x)], axis=-1).astype(BF16)


def kernel(x_prompt, x_sample, state_ret, state_lru, state_conv, w_in, norm_mix, gn_gain, w_ret_out, conv_w,
           conv_b, lru_wa, lru_ba, lru_wx, lru_bx, lru_lambda, w_lru_out, b_gate, w_o, norm_ffn, w_gate_up,
           w_down, norm_final):
    bp, lp, _ = x_prompt.shape
    bs, ls, _ = x_sample.shape
    assert ls == 1 and lp % RET_ROWS == 0 and lp % LRU_STEPS == 0 and bs % (bp * lp // FFN_ROWS) == 0
    depth = w_in.shape[0]

    w_in = w_in.astype(BF16)
    tile_cols = lambda lo: w_in[:, :, lo:lo + LRU_TILE]
    w_xy = jnp.stack([jnp.concatenate([tile_cols(COL_XY + w * LRU_TILE), tile_cols(COL_XY + LRU_WIDTH + w * LRU_TILE)],
                                      axis=-1) for w in range(LRU_NTILES)], axis=1)
    w_gab = w_in[:, :, COL_GAB:]
    w_ret_out, w_lru_out, w_o = w_ret_out.astype(BF16), w_lru_out.astype(BF16), w_o.astype(BF16)
    w_gate_up, w_down = w_gate_up.astype(BF16), w_down.astype(BF16)
    w_bd = _block_diag_gates(lru_wa, lru_wx)

    cos_p, sin_p = _rope_tables(jnp.arange(lp, dtype=F32))
    cos_s, sin_s = _rope_tables(PAST_LEN + jnp.arange(ls, dtype=F32))
    cos_s = jnp.broadcast_to(cos_s, (bs, ROPE_HALF))
    sin_s = jnp.broadcast_to(sin_s, (bs, ROPE_HALF))

    xp = x_prompt.reshape(bp * lp, D_MODEL)
    xs = x_sample.reshape(bs * ls, D_MODEL)
    hp = _rmsnorm(xp, norm_mix[0], BF16, 1024)
    hs = _rmsnorm(xs, norm_mix[0], BF16, bs)
    lru_p, conv_p, lru_s, conv_s = [], [], [], []
    ret_p = ret_s = None
    for l in range(depth):
        last = l == depth - 1
        next_gain = norm_final if last else norm_mix[l + 1]
        lru_args = (conv_w[l], conv_b[l], w_bd, lru_ba[l], lru_bx[l], lru_lambda[l])

        qkv = _qkv_proj(hs, w_in, l, cos_s, sin_s)
        g_ret = _gate_proj(hs, w_in, l)
        state_update = functools.partial(_retention_step_call, qkv, g_ret, gn_gain[l], state_ret, l, ret_s)

        ret_gp, ret_p = _retention_prompt(hp, w_in, cos_p, sin_p, gn_gain[l], bp, lp, l, ret_p)
        lru_g, h_last, x_tail = _lru_prompt(hp.reshape(bp, lp, D_MODEL), w_xy, l, *lru_args)
        lru_p.append(h_last)
        conv_p.append(x_tail.reshape(CONV_W - 1, bp, LRU_WIDTH).transpose(1, 0, 2))
        xp = _merge(ret_gp, lru_g.reshape(bp * lp, LRU_WIDTH), hp, xp, w_ret_out, w_lru_out, w_gab, w_o, l,
                    b_gate[l], 512)
        *out, ret_gs, ret_s = _ffn(xp, norm_ffn[l], w_gate_up, w_down, l, next_gain, last, FFN_ROWS, state_update)
        xp, hp = (None, out[0]) if last else out

        buf2d = state_conv[l].reshape(bs, (CONV_W - 1) * LRU_WIDTH)
        lru_g, h_new, x_lru = _lru_step(hs, w_xy, l, buf2d, state_lru[l], *lru_args)
        lru_s.append(h_new)
        conv_s.append(jnp.concatenate([state_conv[l][:, 1:], x_lru[:, None, :]], axis=1))
        xs = _merge(ret_gs.astype(BF16), lru_g, hs, xs, w_ret_out, w_lru_out, w_gab, w_o, l, b_gate[l], bs)
        out = _ffn(xs, norm_ffn[l], w_gate_up, w_down, l, next_gain, last, bs)
        xs, hs = (None, out[0]) if last else out

    stack = lambda parts: jnp.stack(parts, 0)
    return (hp.reshape(bp, lp, D_MODEL), hs.reshape(bs, ls, D_MODEL), ret_p, stack(lru_p), stack(conv_p),
            ret_s, stack(lru_s), stack(conv_s))
```

```python
import jax
import jax.numpy as jnp
import numpy as np
from jax import lax
from jax.experimental import pallas as pl
from jax.experimental.pallas import tpu as pltpu

F32 = jnp.float32
BF16 = jnp.bfloat16

D_MODEL = 1024
PAST_LEN = 16384
RET_HEADS = 4
RET_DK = 256
RET_DV = 512
RET_QK = RET_HEADS * RET_DK
RET_V = RET_HEADS * RET_DV
RET_CHUNK = 256
ROPE_THETA = 10000.0
ROPE_HALF = RET_DK // 2
LRU_WIDTH = 1280
LRU_BLOCKS = 16
LRU_BLOCK = LRU_WIDTH // LRU_BLOCKS
LRU_C = 8.0
CONV_W = 4
D_FF = 2816
EPS = 1e-6

COL_GATE = 2 * RET_QK + RET_V
COL_XY = COL_GATE + RET_V
COL_GAB = COL_XY + 2 * LRU_WIDTH

V7X_VMEM_BYTES = 64 * 1024 * 1024
VMEM_LIMIT_BYTES = V7X_VMEM_BYTES * 7 // 8
SUBLANES = 8
LANES = 128

LRU_TILE = int(np.lcm(LRU_BLOCK, LANES))
LRU_NTILES = LRU_WIDTH // LRU_TILE
LRU_ROWS = 256
LRU_STEPS = 256
LRU_PITCH = LRU_STEPS + SUBLANES
RET_ROWS = 512
STEP_SEQS = 4
FFN_CHUNK = 256
QKV_TILE = 512


def _params(*semantics):
    return pltpu.CompilerParams(dimension_semantics=semantics, vmem_limit_bytes=VMEM_LIMIT_BYTES)


def _layer_block(shape, layer, col_block=0):
    if callable(col_block):
        return pl.BlockSpec((1,) + shape[1:], lambda *g: (layer, 0, col_block(*g)))
    return pl.BlockSpec((1,) + shape[1:], lambda *g: (layer, 0, col_block))


def _rms(x, gain):
    return x * lax.rsqrt(jnp.mean(x * x, axis=-1, keepdims=True) + EPS) * gain


def _rmsnorm_body(x_ref, g_ref, o_ref):
    o_ref[...] = _rms(x_ref[...], g_ref[...]).astype(o_ref.dtype)


def _rmsnorm(x, gain, out_dtype, tm):
    t, d = x.shape
    return pl.pallas_call(
        _rmsnorm_body,
        grid=(t // tm,),
        in_specs=[pl.BlockSpec((tm, d), lambda i: (i, 0)), pl.BlockSpec((1, d), lambda i: (0, 0))],
        out_specs=pl.BlockSpec((tm, d), lambda i: (i, 0)),
        out_shape=jax.ShapeDtypeStruct((t, d), out_dtype),
        compiler_params=_params("parallel"),
        name="rmsnorm",
    )(x, gain.reshape(1, d))


def _rope_bf16(x, cos, sin, scale):
    x1 = x[:, :ROPE_HALF]
    x2 = x[:, ROPE_HALF:]
    return jnp.concatenate([(x1 * cos - x2 * sin) * scale, (x1 * sin + x2 * cos) * scale], axis=1).astype(BF16)


def _qkv_body(h_ref, w_ref, cos_ref, sin_ref, o_ref):
    acc = jnp.dot(h_ref[...], w_ref[0], preferred_element_type=F32)
    j = pl.program_id(1)
    n_q = RET_QK // QKV_TILE

    @pl.when(j < 2 * n_q)
    def _():
        scale = jnp.where(j < n_q, 1.0, RET_DK ** -0.5).astype(F32)
        for hd in range(QKV_TILE // RET_DK):
            cols = slice(hd * RET_DK, (hd + 1) * RET_DK)
            o_ref[:, cols] = _rope_bf16(acc[:, cols], cos_ref[...], sin_ref[...], scale)

    @pl.when(j >= 2 * n_q)
    def _():
        o_ref[...] = acc.astype(o_ref.dtype)


def _qkv_proj(h, w_in, layer, cos, sin):
    t, d = h.shape
    n = COL_GATE
    return pl.pallas_call(
        _qkv_body,
        grid=(1, n // QKV_TILE),
        in_specs=[
            pl.BlockSpec((t, d), lambda i, j: (0, 0)),
            _layer_block((1, d, QKV_TILE), layer, lambda i, j: j),
            pl.BlockSpec((t, ROPE_HALF), lambda i, j: (0, 0)),
            pl.BlockSpec((t, ROPE_HALF), lambda i, j: (0, 0)),
        ],
        out_specs=pl.BlockSpec((t, QKV_TILE), lambda i, j: (0, j)),
        out_shape=jax.ShapeDtypeStruct((t, n), BF16),
        compiler_params=_params("parallel", "arbitrary"),
        name="qkv_proj",
    )(h, w_in, cos, sin)


def _matmul_body(h_ref, w_ref, o_ref):
    o_ref[...] = jnp.dot(h_ref[...], w_ref[0], preferred_element_type=F32).astype(o_ref.dtype)


def _gate_proj(h, w_in, layer):
    t, d = h.shape
    tn = 512
    first = COL_GATE // tn
    return pl.pallas_call(
        _matmul_body,
        grid=(RET_V // tn,),
        in_specs=[pl.BlockSpec((t, d), lambda j: (0, 0)), _layer_block((1, d, tn), layer, lambda j: first + j)],
        out_specs=pl.BlockSpec((t, tn), lambda j: (0, j)),
        out_shape=jax.ShapeDtypeStruct((t, RET_V), F32),
        compiler_params=_params("arbitrary"),
        name="gret_proj",
    )(h, w_in)


def _group_norm_gate(o, gate, gain):
    mu = jnp.mean(o, axis=-1, keepdims=True)
    var = jnp.mean(jnp.square(o - mu), axis=-1, keepdims=True)
    on = (o - mu) * lax.rsqrt(var + EPS) * gain
    return jax.nn.silu(gate) * on


def _retention_body(h_ref, w_ref, cos_ref, sin_ref, gain_ref, decay_ref, qdec_ref, kdec_ref, cdec_ref,
                    *rest):
    o_ref, s_out_ref, s_ref = rest[-3:]
    blk = pl.program_id(1)

    @pl.when(blk == 0)
    def _():
        s_ref[...] = jnp.zeros_like(s_ref)

    h = h_ref[...]
    cos = cos_ref[...]
    sin = sin_ref[...]
    proj = lambda lo, n: jnp.dot(h, w_ref[0, :, lo:lo + n], preferred_element_type=F32)
    for hd in range(RET_HEADS):
        v_cols = slice(hd * RET_DV, (hd + 1) * RET_DV)
        q = _rope_bf16(proj(hd * RET_DK, RET_DK), cos, sin, 1.0)
        k = _rope_bf16(proj(RET_QK + hd * RET_DK, RET_DK), cos, sin, RET_DK ** -0.5)
        v = proj(2 * RET_QK + hd * RET_DV, RET_DV).astype(BF16)
        gate = proj(COL_GATE + hd * RET_DV, RET_DV)
        s = s_ref[hd]
        for r0 in range(0, h.shape[0], RET_CHUNK):
            rows = slice(r0, r0 + RET_CHUNK)
            qc, kc, vc = q[rows], k[rows], v[rows]
            scores = lax.dot_general(qc, kc, (((1,), (1,)), ((), ())), preferred_element_type=F32) * decay_ref[hd]
            inner = jnp.dot(scores.astype(BF16), vc, preferred_element_type=F32)
            cross = jnp.dot(qc, s.astype(BF16), preferred_element_type=F32) * qdec_ref[hd]
            kd = (kc.astype(F32) * kdec_ref[hd]).astype(BF16)
            s = s * cdec_ref[hd] + lax.dot_general(kd, vc, (((0,), (0,)), ((), ())), preferred_element_type=F32)
            o_ref[rows, v_cols] = _group_norm_gate(inner + cross, gate[rows], gain_ref[:, v_cols]).astype(o_ref.dtype)
        s_ref[hd] = s

    @pl.when(blk == pl.num_programs(1) - 1)
    def _():
        s_out_ref[0, 0] = s_ref[...]


def _retention_tables(chunk):
    log_g = jnp.log(1.0 - 2.0 ** (-5.0 - jnp.arange(RET_HEADS, dtype=F32)))
    idx = jnp.arange(chunk, dtype=F32)
    diff = idx[:, None] - idx[None, :]
    decay = jnp.where(diff >= 0, jnp.exp(log_g[:, None, None] * jnp.maximum(diff, 0.0)), 0.0)
    q_dec = jnp.exp(log_g[:, None] * (idx[None, :] + 1.0))
    k_dec = jnp.exp(log_g[:, None] * (chunk - 1.0 - idx[None, :]))
    chunk_dec = jnp.exp(log_g * chunk)
    return decay, q_dec[:, :, None], k_dec[:, :, None], chunk_dec[:, None, None]


def _retention_prompt(h, w_in, cos, sin, gain, batch, seq, layer, prev_states):
    t, d = h.shape
    decay, q_dec, k_dec, chunk_dec = _retention_tables(RET_CHUNK)
    nblk = seq // RET_ROWS
    full = lambda a: pl.BlockSpec(a.shape, lambda b, j: (0,) * a.ndim)
    pos = pl.BlockSpec((RET_ROWS, ROPE_HALF), lambda b, j: (j, 0))
    state_shape = (w_in.shape[0], batch, RET_HEADS, RET_DK, RET_DV)
    operands = [h, w_in, cos, sin, gain.reshape(1, RET_V), decay, q_dec, k_dec, chunk_dec]
    in_specs = [pl.BlockSpec((RET_ROWS, d), lambda b, j: (b * nblk + j, 0)),
                _layer_block((1, d, COL_XY), layer), pos, pos,
                full(operands[4]), full(decay), full(q_dec), full(k_dec), full(chunk_dec)]
    aliases = {}
    if prev_states is not None:
        aliases = {len(operands): 1}
        operands.append(prev_states)
        in_specs.append(pl.BlockSpec(memory_space=pl.ANY))
    return pl.pallas_call(
        _retention_body,
        grid=(batch, nblk),
        in_specs=in_specs,
        out_specs=[
            pl.BlockSpec((RET_ROWS, RET_V), lambda b, j: (b * nblk + j, 0)),
            pl.BlockSpec((1, 1) + state_shape[2:], lambda b, j: (layer, b, 0, 0, 0)),
        ],
        out_shape=[jax.ShapeDtypeStruct((t, RET_V), BF16), jax.ShapeDtypeStruct(state_shape, F32)],
        scratch_shapes=[pltpu.VMEM(state_shape[2:], F32)],
        input_output_aliases=aliases,
        compiler_params=_params("parallel", "arbitrary"),
        name="retention_prompt",
    )(*operands)


def _retention_step_body(qt_ref, kt_ref, q_ref, k_ref, v_ref, g_ref, gain_ref, dec_ref, s_ref, *rest):
    o_ref, s_out_ref = rest[-2:]
    n_seq = qt_ref.shape[1]
    seq_ids = lax.broadcasted_iota(jnp.int32, (n_seq, RET_DV), 0)
    for n in range(STEP_SEQS):
        seq = pl.program_id(0) * STEP_SEQS + n
        row = pl.ds(seq, 1)
        one_hot = (seq_ids == seq).astype(BF16)
        for hd in range(RET_HEADS):
            qk_cols = slice(hd * RET_DK, (hd + 1) * RET_DK)
            cols = slice(hd * RET_DV, (hd + 1) * RET_DV)
            q_spread = jnp.dot(qt_ref[qk_cols, :], one_hot, preferred_element_type=F32)
            k_spread = jnp.dot(kt_ref[qk_cols, :], one_hot, preferred_element_type=F32)
            s = s_ref[0, n, hd]
            v = v_ref[row, cols]
            decay = dec_ref[hd:hd + 1, 0:1]
            q_dec = dec_ref[hd:hd + 1, 1:2]
            k_dec = dec_ref[hd:hd + 1, 2:3]
            chunk_dec = dec_ref[hd:hd + 1, 3:4]
            scores = jnp.sum(q_ref[row, qk_cols] * k_ref[row, qk_cols], axis=1, keepdims=True) * decay
            cross = jnp.sum(q_spread * s, axis=0, keepdims=True) * q_dec
            s_out_ref[0, n, hd] = s * chunk_dec + k_spread * (k_dec * v)
            o = scores * v + cross
            o_ref[row, cols] = _group_norm_gate(o, g_ref[row, cols], gain_ref[:, cols])


def _retention_step(qkv, g_ret, gain, states, layer, prev_states):
    batch = qkv.shape[0]
    decay, q_dec, k_dec, chunk_dec = _retention_tables(1)
    dec = jnp.concatenate([decay[:, :, 0], q_dec[:, :, 0], k_dec[:, :, 0], chunk_dec[:, :, 0]], axis=1)
    q = qkv[:, :RET_QK]
    k = qkv[:, RET_QK:2 * RET_QK]
    full = lambda a: pl.BlockSpec(a.shape, lambda i: (0, 0))
    st = pl.BlockSpec((1, STEP_SEQS) + states.shape[2:], lambda i: (layer, i, 0, 0, 0))
    operands = [q.T, k.T, q.astype(F32), k.astype(F32), qkv[:, 2 * RET_QK:].astype(F32), g_ret,
                gain.reshape(1, RET_V), dec]
    in_specs = [full(a) for a in operands] + [st]
    operands.append(states)
    aliases = {}
    if prev_states is not None:
        aliases = {len(operands): 1}
        operands.append(prev_states)
        in_specs.append(pl.BlockSpec(memory_space=pl.ANY))
    return pl.pallas_call(
        _retention_step_body,
        grid=(batch // STEP_SEQS,),
        in_specs=in_specs,
        out_specs=[pl.BlockSpec((batch, RET_V), lambda i: (0, 0)), st],
        out_shape=[jax.ShapeDtypeStruct((batch, RET_V), F32), jax.ShapeDtypeStruct(states.shape, F32)],
        input_output_aliases=aliases,
        compiler_params=_params("arbitrary"),
        name="retention_step",
    )(*operands)


def _lru_gates(xc, w_ref, ba_ref, bx_ref, lam_ref):
    g = jnp.dot(xc.astype(BF16), w_ref[0, 0], preferred_element_type=F32)
    r = jax.nn.sigmoid(g[:, :LRU_TILE] + ba_ref[...])
    i = jax.nn.sigmoid(g[:, LRU_TILE:] + bx_ref[...])
    log_a = -LRU_C * r * jax.nn.softplus(-lam_ref[...])
    a = jnp.exp(log_a)
    th = jnp.tanh(log_a)
    u = jnp.sqrt(-2.0 * th / (1.0 - th)) * (i * xc)
    return a, u


def _lru_prompt_body(h_ref, wxy_ref, cw_ref, cb_ref, w_ref, ba_ref, bx_ref, lam_ref,
                     o_ref, hl_ref, xt_ref, xs_ref, hs_ref, y_ref, xpad_ref, state_ref):
    n_seq, steps, _ = h_ref.shape
    tw = o_ref.shape[2]
    slabs = tw // LANES
    tail = (CONV_W - 1) * SUBLANES
    sub_steps = LRU_ROWS // SUBLANES
    blk = pl.program_id(1)

    @pl.when(blk == 0)
    def _():
        xpad_ref[0:tail, :] = jnp.zeros((tail, tw), F32)
        state_ref[...] = jnp.zeros_like(state_ref)

    for b in range(n_seq):
        xy = jnp.dot(h_ref[b], wxy_ref[0, 0], preferred_element_type=F32)
        for s in range(slabs):
            xs_ref[s, b * LRU_PITCH:b * LRU_PITCH + steps, :] = xy[:, s * LANES:(s + 1) * LANES]
        y_ref[b] = xy[:, tw:]

    h = state_ref[...]
    for t0 in range(0, steps, sub_steps):
        for i in range(sub_steps):
            for s in range(slabs):
                xpad_ref[tail + i * SUBLANES:tail + (i + 1) * SUBLANES, s * LANES:(s + 1) * LANES] = (
                    xs_ref.at[s][pl.ds(t0 + i, n_seq, stride=LRU_PITCH), :])
        xc = cb_ref[...]
        for j in range(CONV_W):
            xc = xc + xpad_ref[j * SUBLANES:j * SUBLANES + LRU_ROWS, :] * cw_ref[j:j + 1, :]
        a, u = _lru_gates(xc, w_ref, ba_ref, bx_ref, lam_ref)
        for i in range(sub_steps):
            g = i * SUBLANES
            h = a[g:g + SUBLANES, :] * h + u[g:g + SUBLANES, :]
            for s in range(slabs):
                hs_ref.at[s][pl.ds(t0 + i, n_seq, stride=LRU_PITCH), :] = h[:, s * LANES:(s + 1) * LANES]
        xpad_ref[0:tail, :] = xpad_ref[LRU_ROWS:LRU_ROWS + tail, :]
    state_ref[...] = h

    for b in range(n_seq):
        plane = slice(b * LRU_PITCH, b * LRU_PITCH + steps)
        h_seq = jnp.concatenate([hs_ref[s, plane, :] for s in range(slabs)], axis=1)
        o_ref[b] = (jax.nn.gelu(y_ref[b]) * h_seq).astype(o_ref.dtype)

    @pl.when(blk == pl.num_programs(1) - 1)
    def _():
        hl_ref[...] = h
        xt_ref[...] = xpad_ref[0:tail, :]


def _lru_specs(rows, index):
    return pl.BlockSpec((rows, LRU_TILE), index)


def _lru_prompt(h, w_xy, layer, conv_w, conv_b, w_bd, b_a, b_x, lam):
    batch, seq, d = h.shape
    assert batch == SUBLANES
    tail = (CONV_W - 1) * batch
    vec = lambda r: _lru_specs(r, lambda w, t: (0, w))
    staging = pltpu.VMEM((LRU_TILE // LANES, batch * LRU_PITCH, LANES), F32)
    return pl.pallas_call(
        _lru_prompt_body,
        grid=(LRU_NTILES, seq // LRU_STEPS),
        in_specs=[
            pl.BlockSpec((batch, LRU_STEPS, d), lambda w, t: (0, t, 0)),
            pl.BlockSpec((1, 1) + w_xy.shape[2:], lambda w, t: (layer, w, 0, 0)),
            vec(CONV_W), vec(1),
            pl.BlockSpec((1, 1) + w_bd.shape[2:], lambda w, t: (layer, w, 0, 0)),
            vec(1), vec(1), vec(1),
        ],
        out_specs=[pl.BlockSpec((batch, LRU_STEPS, LRU_TILE), lambda w, t: (0, t, w)), vec(batch), vec(tail)],
        out_shape=[
            jax.ShapeDtypeStruct((batch, seq, LRU_WIDTH), BF16),
            jax.ShapeDtypeStruct((batch, LRU_WIDTH), F32),
            jax.ShapeDtypeStruct((tail, LRU_WIDTH), F32),
        ],
        scratch_shapes=[
            staging, staging,
            pltpu.VMEM((batch, LRU_STEPS, LRU_TILE), F32),
            pltpu.VMEM((tail + LRU_ROWS, LRU_TILE), F32),
            pltpu.VMEM((batch, LRU_TILE), F32),
        ],
        compiler_params=_params("parallel", "arbitrary"),
        name="lru_prompt",
    )(h, w_xy, conv_w, conv_b.reshape(1, -1), w_bd, b_a.reshape(1, -1), b_x.reshape(1, -1), lam.reshape(1, -1))


def _lru_step_body(h_ref, wxy_ref, b0_ref, b1_ref, b2_ref, h0_ref, cw_ref, cb_ref, w_ref, ba_ref, bx_ref,
                   lam_ref, o_ref, hn_ref, x_ref):
    tw = o_ref.shape[1]
    xy = jnp.dot(h_ref[...], wxy_ref[0, 0], preferred_element_type=F32)
    x = xy[:, :tw]
    x_ref[...] = x
    taps = (b0_ref[...], b1_ref[...], b2_ref[...], x)
    xc = cb_ref[...]
    for j in range(CONV_W):
        xc = xc + taps[j] * cw_ref[j:j + 1, :]
    a, u = _lru_gates(xc, w_ref, ba_ref, bx_ref, lam_ref)
    h = u + a * h0_ref[...]
    hn_ref[...] = h
    o_ref[...] = (jax.nn.gelu(xy[:, tw:]) * h).astype(o_ref.dtype)


def _lru_step(h, w_xy, layer, conv_buf, h0, conv_w, conv_b, w_bd, b_a, b_x, lam):
    batch, d = h.shape
    tile = lambda off: _lru_specs(batch, lambda w: (0, off + w))
    vec = lambda rows: _lru_specs(rows, lambda w: (0, w))
    f32_out = jax.ShapeDtypeStruct((batch, LRU_WIDTH), F32)
    return pl.pallas_call(
        _lru_step_body,
        grid=(LRU_NTILES,),
        in_specs=[
            pl.BlockSpec(h.shape, lambda w: (0, 0)),
            pl.BlockSpec((1, 1) + w_xy.shape[2:], lambda w: (layer, w, 0, 0)),
            tile(0), tile(LRU_NTILES), tile(2 * LRU_NTILES),
            tile(0),
            vec(CONV_W), vec(1),
            pl.BlockSpec((1, 1) + w_bd.shape[2:], lambda w: (layer, w, 0, 0)),
            vec(1), vec(1), vec(1),
        ],
        out_specs=[tile(0), tile(0), tile(0)],
        out_shape=[jax.ShapeDtypeStruct((batch, LRU_WIDTH), BF16), f32_out, f32_out],
        compiler_params=_params("parallel"),
        name="lru_step",
    )(h, w_xy, conv_buf, conv_buf, conv_buf, h0, conv_w, conv_b.reshape(1, -1), w_bd,
      b_a.reshape(1, -1), b_x.reshape(1, -1), lam.reshape(1, -1))


def _merge_body(ret_ref, lru_ref, h_ref, x_ref, wr_ref, wl_ref, wg_ref, wo_ref, bg_ref, o_ref):
    h = h_ref[...]
    gate_a = jax.nn.sigmoid(jnp.dot(h, wg_ref[0, :, :D_MODEL], preferred_element_type=F32) + bg_ref[0:1, :])
    merged = gate_a * jnp.dot(ret_ref[...], wr_ref[0], preferred_element_type=F32)
    gate_b = jax.nn.sigmoid(jnp.dot(h, wg_ref[0, :, D_MODEL:], preferred_element_type=F32) + bg_ref[1:2, :])
    merged = merged + gate_b * jnp.dot(lru_ref[...], wl_ref[0], preferred_element_type=F32)
    o_ref[...] = x_ref[...] + jnp.dot(merged.astype(BF16), wo_ref[0], preferred_element_type=F32)


def _merge(ret_g, lru_g, h, x, w_ret, w_lru, w_gab, w_o, layer, b_gate, tm):
    t = x.shape[0]
    rows = lambda n: pl.BlockSpec((tm, n), lambda i: (i, 0))
    whole = lambda w: _layer_block(w.shape, layer)
    return pl.pallas_call(
        _merge_body,
        grid=(t // tm,),
        in_specs=[rows(RET_V), rows(LRU_WIDTH), rows(D_MODEL), rows(D_MODEL),
                  whole(w_ret), whole(w_lru), whole(w_gab), whole(w_o), pl.BlockSpec(b_gate.shape, lambda i: (0, 0))],
        out_specs=rows(D_MODEL),
        out_shape=jax.ShapeDtypeStruct((t, D_MODEL), F32),
        compiler_params=_params("parallel"),
        name="merge_out",
    )(ret_g, lru_g, h, x, w_ret, w_lru, w_gab, w_o, b_gate)


def _ffn_body(x_ref, g_ref, wgu_ref, wd_ref, gn_ref, *out_refs):
    x = x_ref[...]
    h = _rms(x, g_ref[...]).astype(BF16)
    acc = jnp.zeros(x.shape, F32)
    for c in range(0, D_FF, FFN_CHUNK):
        gate = jnp.dot(h, wgu_ref[0, :, c:c + FFN_CHUNK], preferred_element_type=F32)
        up = jnp.dot(h, wgu_ref[0, :, D_FF + c:D_FF + c + FFN_CHUNK], preferred_element_type=F32)
        act = (jax.nn.silu(gate) * up).astype(BF16)
        acc = acc + jnp.dot(act, wd_ref[0, c:c + FFN_CHUNK, :], preferred_element_type=F32)
    x_new = x + acc
    if len(out_refs) == 2:
        out_refs[0][...] = x_new
    out_refs[-1][...] = _rms(x_new, gn_ref[...]).astype(out_refs[-1].dtype)


def _ffn(x, gain, w_gate_up, w_down, layer, next_gain, last, tm):
    t = x.shape[0]
    rows = pl.BlockSpec((tm, D_MODEL), lambda i: (i, 0))
    vec = pl.BlockSpec((1, D_MODEL), lambda i: (0, 0))
    if last:
        out_specs, out_shape = [rows], [jax.ShapeDtypeStruct((t, D_MODEL), F32)]
    else:
        out_specs = [rows, rows]
        out_shape = [jax.ShapeDtypeStruct((t, D_MODEL), F32), jax.ShapeDtypeStruct((t, D_MODEL), BF16)]
    return pl.pallas_call(
        _ffn_body,
        grid=(t // tm,),
        in_specs=[rows, vec, _layer_block(w_gate_up.shape, layer), _layer_block(w_down.shape, layer), vec],
        out_specs=out_specs,
        out_shape=out_shape,
        compiler_params=_params("parallel"),
        name="swiglu",
    )(x, gain.reshape(1, D_MODEL), w_gate_up, w_down, next_gain.reshape(1, D_MODEL))


def _rope_tables(pos):
    freqs = ROPE_THETA ** (-jnp.arange(ROPE_HALF, dtype=F32) / ROPE_HALF)
    ang = pos[:, None] * freqs[None, :]
    return jnp.cos(ang), jnp.sin(ang)


def _block_diag_gates(w_a, w_x):
    depth = w_a.shape[0]
    per_tile = LRU_TILE // LRU_BLOCK
    eye = jnp.eye(per_tile, dtype=w_a.dtype)

    def dense(w):
        w = w.reshape(depth, LRU_NTILES, per_tile, LRU_BLOCK, LRU_BLOCK)
        return jnp.einsum("ltncd,nm->ltncmd", w, eye).reshape(depth, LRU_NTILES, LRU_TILE, LRU_TILE)

    return jnp.concatenate([dense(w_a), dense(w_x)], axis=-1).astype(BF16)


def kernel(x_prompt, x_sample, state_ret, state_lru, state_conv, w_in, norm_mix, gn_gain, w_ret_out, conv_w,
           conv_b, lru_wa, lru_ba, lru_wx, lru_bx, lru_lambda, w_lru_out, b_gate, w_o, norm_ffn, w_gate_up,
           w_down, norm_final):
    bp, lp, _ = x_prompt.shape
    bs, ls, _ = x_sample.shape
    assert ls == 1 and lp % RET_ROWS == 0 and lp % LRU_STEPS == 0 and bs % STEP_SEQS == 0
    depth = w_in.shape[0]

    w_in = w_in.astype(BF16)
    tile_cols = lambda lo: w_in[:, :, lo:lo + LRU_TILE]
    w_xy = jnp.stack([jnp.concatenate([tile_cols(COL_XY + w * LRU_TILE), tile_cols(COL_XY + LRU_WIDTH + w * LRU_TILE)],
                                      axis=-1) for w in range(LRU_NTILES)], axis=1)
    w_gab = w_in[:, :, COL_GAB:]
    w_ret_out, w_lru_out, w_o = w_ret_out.astype(BF16), w_lru_out.astype(BF16), w_o.astype(BF16)
    w_gate_up, w_down = w_gate_up.astype(BF16), w_down.astype(BF16)
    w_bd = _block_diag_gates(lru_wa, lru_wx)

    cos_p, sin_p = _rope_tables(jnp.arange(lp, dtype=F32))
    cos_s, sin_s = _rope_tables(PAST_LEN + jnp.arange(ls, dtype=F32))
    cos_s = jnp.broadcast_to(cos_s, (bs, ROPE_HALF))
    sin_s = jnp.broadcast_to(sin_s, (bs, ROPE_HALF))

    xp = x_prompt.reshape(bp * lp, D_MODEL)
    xs = x_sample.reshape(bs * ls, D_MODEL)
    hp = _rmsnorm(xp, norm_mix[0], BF16, 1024)
    hs = _rmsnorm(xs, norm_mix[0], BF16, bs)
    lru_p, conv_p, lru_s, conv_s = [], [], [], []
    ret_p = ret_s = None
    for l in range(depth):
        last = l == depth - 1
        next_gain = norm_final if last else norm_mix[l + 1]
        lru_args = (conv_w[l], conv_b[l], w_bd, lru_ba[l], lru_bx[l], lru_lambda[l])

        ret_g, ret_p = _retention_prompt(hp, w_in, cos_p, sin_p, gn_gain[l], bp, lp, l, ret_p)
        lru_g, h_last, x_tail = _lru_prompt(hp.reshape(bp, lp, D_MODEL), w_xy, l, *lru_args)
        lru_p.append(h_last)
        conv_p.append(x_tail.reshape(CONV_W - 1, bp, LRU_WIDTH).transpose(1, 0, 2))
        xp = _merge(ret_g, lru_g.reshape(bp * lp, LRU_WIDTH), hp, xp, w_ret_out, w_lru_out, w_gab, w_o, l,
                    b_gate[l], 512)
        out = _ffn(xp, norm_ffn[l], w_gate_up, w_down, l, next_gain, last, 512)
        xp, hp = (None, out[0]) if last else out

        qkv = _qkv_proj(hs, w_in, l, cos_s, sin_s)
        g_ret = _gate_proj(hs, w_in, l)
        ret_g, ret_s = _retention_step(qkv, g_ret, gn_gain[l], state_ret, l, ret_s)
        buf2d = state_conv[l].reshape(bs, (CONV_W - 1) * LRU_WIDTH)
        lru_g, h_new, x_lru = _lru_step(hs, w_xy, l, buf2d, state_lru[l], *lru_args)
        lru_s.append(h_new)
        conv_s.append(jnp.concatenate([state_conv[l][:, 1:], x_lru[:, None, :]], axis=1))
        xs = _merge(ret_g.astype(BF16), lru_g, hs, xs, w_ret_out, w_lru_out, w_gab, w_o, l, b_gate[l], bs)
        out = _ffn(xs, norm_ffn[l], w_gate_up, w_down, l, next_gain, last, bs)
        xs, hs = (None, out[0]) if last else out

    stack = lambda parts: jnp.stack(parts, 0)
    return (hp.reshape(bp, lp, D_MODEL), hs.reshape(bs, ls, D_MODEL), ret_p, stack(lru_p), stack(conv_p),
            ret_s, stack(lru_s), stack(conv_s))
```

```python
import jax
import jax.numpy as jnp
import numpy as np
from jax import lax
from jax.experimental import pallas as pl
from jax.experimental.pallas import tpu as pltpu

F32 = jnp.float32
BF16 = jnp.bfloat16

D_MODEL = 1024
PAST_LEN = 16384
RET_HEADS = 4
RET_DK = 256
RET_DV = 512
RET_QK = RET_HEADS * RET_DK
RET_V = RET_HEADS * RET_DV
RET_CHUNK = 256
ROPE_THETA = 10000.0
ROPE_HALF = RET_DK // 2
LRU_WIDTH = 1280
LRU_BLOCKS = 16
LRU_BLOCK = LRU_WIDTH // LRU_BLOCKS
LRU_C = 8.0
CONV_W = 4
D_FF = 2816
EPS = 1e-6

COL_GATE = 2 * RET_QK + RET_V
COL_XY = COL_GATE + RET_V
COL_GAB = COL_XY + 2 * LRU_WIDTH

V7X_VMEM_BYTES = 64 * 1024 * 1024
VMEM_LIMIT_BYTES = V7X_VMEM_BYTES * 7 // 8
SUBLANES = 8
LANES = 128

LRU_TILE = int(np.lcm(LRU_BLOCK, LANES))
LRU_NTILES = LRU_WIDTH // LRU_TILE
LRU_ROWS = 256
LRU_STEPS = 256
LRU_PITCH = LRU_STEPS + SUBLANES
RET_ROWS = 512
STEP_SEQS = 4
FFN_CHUNK = 256
QKV_TILE = 512


def _params(*semantics):
    return pltpu.CompilerParams(dimension_semantics=semantics, vmem_limit_bytes=VMEM_LIMIT_BYTES)


def _layer_block(shape, layer, col_block=0):
    if callable(col_block):
        return pl.BlockSpec((1,) + shape[1:], lambda *g: (layer, 0, col_block(*g)))
    return pl.BlockSpec((1,) + shape[1:], lambda *g: (layer, 0, col_block))


def _rms(x, gain):
    return x * lax.rsqrt(jnp.mean(x * x, axis=-1, keepdims=True) + EPS) * gain


def _rmsnorm_body(x_ref, g_ref, o_ref):
    o_ref[...] = _rms(x_ref[...], g_ref[...]).astype(o_ref.dtype)


def _rmsnorm(x, gain, out_dtype, tm):
    t, d = x.shape
    return pl.pallas_call(
        _rmsnorm_body,
        grid=(t // tm,),
        in_specs=[pl.BlockSpec((tm, d), lambda i: (i, 0)), pl.BlockSpec((1, d), lambda i: (0, 0))],
        out_specs=pl.BlockSpec((tm, d), lambda i: (i, 0)),
        out_shape=jax.ShapeDtypeStruct((t, d), out_dtype),
        compiler_params=_params("parallel"),
        name="rmsnorm",
    )(x, gain.reshape(1, d))


def _rope_bf16(x, cos, sin, scale):
    x1 = x[:, :ROPE_HALF]
    x2 = x[:, ROPE_HALF:]
    return jnp.concatenate([(x1 * cos - x2 * sin) * scale, (x1 * sin + x2 * cos) * scale], axis=1).astype(BF16)


def _qkv_body(h_ref, w_ref, cos_ref, sin_ref, o_ref):
    acc = jnp.dot(h_ref[...], w_ref[0], preferred_element_type=F32)
    j = pl.program_id(1)
    n_q = RET_QK // QKV_TILE

    @pl.when(j < 2 * n_q)
    def _():
        scale = jnp.where(j < n_q, 1.0, RET_DK ** -0.5).astype(F32)
        for hd in range(QKV_TILE // RET_DK):
            cols = slice(hd * RET_DK, (hd + 1) * RET_DK)
            o_ref[:, cols] = _rope_bf16(acc[:, cols], cos_ref[...], sin_ref[...], scale)

    @pl.when(j >= 2 * n_q)
    def _():
        o_ref[...] = acc.astype(o_ref.dtype)


def _qkv_proj(h, w_in, layer, cos, sin):
    t, d = h.shape
    n = COL_GATE
    return pl.pallas_call(
        _qkv_body,
        grid=(1, n // QKV_TILE),
        in_specs=[
            pl.BlockSpec((t, d), lambda i, j: (0, 0)),
            _layer_block((1, d, QKV_TILE), layer, lambda i, j: j),
            pl.BlockSpec((t, ROPE_HALF), lambda i, j: (0, 0)),
            pl.BlockSpec((t, ROPE_HALF), lambda i, j: (0, 0)),
        ],
        out_specs=pl.BlockSpec((t, QKV_TILE), lambda i, j: (0, j)),
        out_shape=jax.ShapeDtypeStruct((t, n), BF16),
        compiler_params=_params("parallel", "arbitrary"),
        name="qkv_proj",
    )(h, w_in, cos, sin)


def _matmul_body(h_ref, w_ref, o_ref):
    o_ref[...] = jnp.dot(h_ref[...], w_ref[0], preferred_element_type=F32).astype(o_ref.dtype)


def _gate_proj(h, w_in, layer):
    t, d = h.shape
    tn = 512
    first = COL_GATE // tn
    return pl.pallas_call(
        _matmul_body,
        grid=(RET_V // tn,),
        in_specs=[pl.BlockSpec((t, d), lambda j: (0, 0)), _layer_block((1, d, tn), layer, lambda j: first + j)],
        out_specs=pl.BlockSpec((t, tn), lambda j: (0, j)),
        out_shape=jax.ShapeDtypeStruct((t, RET_V), F32),
        compiler_params=_params("arbitrary"),
        name="gret_proj",
    )(h, w_in)


def _group_norm_gate(o, gate, gain):
    mu = jnp.mean(o, axis=-1, keepdims=True)
    var = jnp.mean(jnp.square(o - mu), axis=-1, keepdims=True)
    on = (o - mu) * lax.rsqrt(var + EPS) * gain
    return jax.nn.silu(gate) * on


def _retention_body(h_ref, w_ref, cos_ref, sin_ref, gain_ref, decay_ref, qdec_ref, kdec_ref, cdec_ref,
                    *rest):
    o_ref, s_out_ref, s_ref = rest[-3:]
    blk = pl.program_id(1)

    @pl.when(blk == 0)
    def _():
        s_ref[...] = jnp.zeros_like(s_ref)

    h = h_ref[...]
    cos = cos_ref[...]
    sin = sin_ref[...]
    proj = lambda lo, n: jnp.dot(h, w_ref[0, :, lo:lo + n], preferred_element_type=F32)
    for hd in range(RET_HEADS):
        v_cols = slice(hd * RET_DV, (hd + 1) * RET_DV)
        q = _rope_bf16(proj(hd * RET_DK, RET_DK), cos, sin, 1.0)
        k = _rope_bf16(proj(RET_QK + hd * RET_DK, RET_DK), cos, sin, RET_DK ** -0.5)
        v = proj(2 * RET_QK + hd * RET_DV, RET_DV).astype(BF16)
        gate = proj(COL_GATE + hd * RET_DV, RET_DV)
        s = s_ref[hd]
        for r0 in range(0, h.shape[0], RET_CHUNK):
            rows = slice(r0, r0 + RET_CHUNK)
            qc, kc, vc = q[rows], k[rows], v[rows]
            scores = lax.dot_general(qc, kc, (((1,), (1,)), ((), ())), preferred_element_type=F32) * decay_ref[hd]
            inner = jnp.dot(scores.astype(BF16), vc, preferred_element_type=F32)
            cross = jnp.dot(qc, s.astype(BF16), preferred_element_type=F32) * qdec_ref[hd]
            kd = (kc.astype(F32) * kdec_ref[hd]).astype(BF16)
            s = s * cdec_ref[hd] + lax.dot_general(kd, vc, (((0,), (0,)), ((), ())), preferred_element_type=F32)
            o_ref[rows, v_cols] = _group_norm_gate(inner + cross, gate[rows], gain_ref[:, v_cols]).astype(o_ref.dtype)
        s_ref[hd] = s

    @pl.when(blk == pl.num_programs(1) - 1)
    def _():
        s_out_ref[0, 0] = s_ref[...]


def _retention_tables(chunk):
    log_g = jnp.log(1.0 - 2.0 ** (-5.0 - jnp.arange(RET_HEADS, dtype=F32)))
    idx = jnp.arange(chunk, dtype=F32)
    diff = idx[:, None] - idx[None, :]
    decay = jnp.where(diff >= 0, jnp.exp(log_g[:, None, None] * jnp.maximum(diff, 0.0)), 0.0)
    q_dec = jnp.exp(log_g[:, None] * (idx[None, :] + 1.0))
    k_dec = jnp.exp(log_g[:, None] * (chunk - 1.0 - idx[None, :]))
    chunk_dec = jnp.exp(log_g * chunk)
    return decay, q_dec[:, :, None], k_dec[:, :, None], chunk_dec[:, None, None]


def _retention_prompt(h, w_in, cos, sin, gain, batch, seq, layer, prev_states):
    t, d = h.shape
    decay, q_dec, k_dec, chunk_dec = _retention_tables(RET_CHUNK)
    nblk = seq // RET_ROWS
    full = lambda a: pl.BlockSpec(a.shape, lambda b, j: (0,) * a.ndim)
    pos = pl.BlockSpec((RET_ROWS, ROPE_HALF), lambda b, j: (j, 0))
    state_shape = (w_in.shape[0], batch, RET_HEADS, RET_DK, RET_DV)
    operands = [h, w_in, cos, sin, gain.reshape(1, RET_V), decay, q_dec, k_dec, chunk_dec]
    in_specs = [pl.BlockSpec((RET_ROWS, d), lambda b, j: (b * nblk + j, 0)),
                _layer_block((1, d, COL_XY), layer), pos, pos,
                full(operands[4]), full(decay), full(q_dec), full(k_dec), full(chunk_dec)]
    aliases = {}
    if prev_states is not None:
        aliases = {len(operands): 1}
        operands.append(prev_states)
        in_specs.append(pl.BlockSpec(memory_space=pl.ANY))
    return pl.pallas_call(
        _retention_body,
        grid=(batch, nblk),
        in_specs=in_specs,
        out_specs=[
            pl.BlockSpec((RET_ROWS, RET_V), lambda b, j: (b * nblk + j, 0)),
            pl.BlockSpec((1, 1) + state_shape[2:], lambda b, j: (layer, b, 0, 0, 0)),
        ],
        out_shape=[jax.ShapeDtypeStruct((t, RET_V), BF16), jax.ShapeDtypeStruct(state_shape, F32)],
        scratch_shapes=[pltpu.VMEM(state_shape[2:], F32)],
        input_output_aliases=aliases,
        compiler_params=_params("parallel", "arbitrary"),
        name="retention_prompt",
    )(*operands)


def _retention_step_body(qt_ref, kt_ref, q_ref, k_ref, v_ref, g_ref, gain_ref, dec_ref, s_ref, *rest):
    o_ref, s_out_ref = rest[-2:]
    n_seq = qt_ref.shape[1]
    seq_ids = lax.broadcasted_iota(jnp.int32, (n_seq, RET_DV), 0)
    for n in range(STEP_SEQS):
        seq = pl.program_id(0) * STEP_SEQS + n
        row = pl.ds(seq, 1)
        one_hot = (seq_ids == seq).astype(BF16)
        for hd in range(RET_HEADS):
            qk_cols = slice(hd * RET_DK, (hd + 1) * RET_DK)
            cols = slice(hd * RET_DV, (hd + 1) * RET_DV)
            q_spread = jnp.dot(qt_ref[qk_cols, :], one_hot, preferred_element_type=F32)
            k_spread = jnp.dot(kt_ref[qk_cols, :], one_hot, preferred_element_type=F32)
            s = s_ref[0, n, hd]
            v = v_ref[row, cols]
            decay = dec_ref[hd:hd + 1, 0:1]
            q_dec = dec_ref[hd:hd + 1, 1:2]
            k_dec = dec_ref[hd:hd + 1, 2:3]
            chunk_dec = dec_ref[hd:hd + 1, 3:4]
            scores = jnp.sum(q_ref[row, qk_cols] * k_ref[row, qk_cols], axis=1, keepdims=True) * decay
            cross = jnp.sum(q_spread * s, axis=0, keepdims=True) * q_dec
            s_out_ref[0, n, hd] = s * chunk_dec + k_spread * (k_dec * v)
            o = scores * v + cross
            o_ref[row, cols] = _group_norm_gate(o, g_ref[row, cols], gain_ref[:, cols])


def _retention_step(qkv, g_ret, gain, states, layer, prev_states):
    batch = qkv.shape[0]
    decay, q_dec, k_dec, chunk_dec = _retention_tables(1)
    dec = jnp.concatenate([decay[:, :, 0], q_dec[:, :, 0], k_dec[:, :, 0], chunk_dec[:, :, 0]], axis=1)
    q = qkv[:, :RET_QK]
    k = qkv[:, RET_QK:2 * RET_QK]
    full = lambda a: pl.BlockSpec(a.shape, lambda i: (0, 0))
    st = pl.BlockSpec((1, STEP_SEQS) + states.shape[2:], lambda i: (layer, i, 0, 0, 0))
    operands = [q.T, k.T, q.astype(F32), k.astype(F32), qkv[:, 2 * RET_QK:].astype(F32), g_ret,
                gain.reshape(1, RET_V), dec]
    in_specs = [full(a) for a in operands] + [st]
    operands.append(states)
    aliases = {}
    if prev_states is not None:
        aliases = {len(operands): 1}
        operands.append(prev_states)
        in_specs.append(pl.BlockSpec(memory_space=pl.ANY))
    return pl.pallas_call(
        _retention_step_body,
        grid=(batch // STEP_SEQS,),
        in_specs=in_specs,
        out_specs=[pl.BlockSpec((batch, RET_V), lambda i: (0, 0)), st],
        out_shape=[jax.ShapeDtypeStruct((batch, RET_V), F32), jax.ShapeDtypeStruct(states.shape, F32)],
        input_output_aliases=aliases,
        compiler_params=_params("arbitrary"),
        name="retention_step",
    )(*operands)


def _lru_gates(xc, w_ref, ba_ref, bx_ref, lam_ref):
    g = jnp.dot(xc.astype(BF16), w_ref[0, 0], preferred_element_type=F32)
    r = jax.nn.sigmoid(g[:, :LRU_TILE] + ba_ref[...])
    i = jax.nn.sigmoid(g[:, LRU_TILE:] + bx_ref[...])
    log_a = -LRU_C * r * jax.nn.softplus(-lam_ref[...])
    a = jnp.exp(log_a)
    th = jnp.tanh(log_a)
    u = jnp.sqrt(-2.0 * th / (1.0 - th)) * (i * xc)
    return a, u


def _lru_prompt_body(h_ref, wxy_ref, cw_ref, cb_ref, w_ref, ba_ref, bx_ref, lam_ref,
                     o_ref, hl_ref, xt_ref, xs_ref, hs_ref, y_ref, xpad_ref, state_ref):
    n_seq, steps, _ = h_ref.shape
    tw = o_ref.shape[2]
    slabs = tw // LANES
    tail = (CONV_W - 1) * SUBLANES
    sub_steps = LRU_ROWS // SUBLANES
    blk = pl.program_id(1)

    @pl.when(blk == 0)
    def _():
        xpad_ref[0:tail, :] = jnp.zeros((tail, tw), F32)
        state_ref[...] = jnp.zeros_like(state_ref)

    for b in range(n_seq):
        xy = jnp.dot(h_ref[b], wxy_ref[0, 0], preferred_element_type=F32)
        for s in range(slabs):
            xs_ref[s, b * LRU_PITCH:b * LRU_PITCH + steps, :] = xy[:, s * LANES:(s + 1) * LANES]
        y_ref[b] = xy[:, tw:]

    h = state_ref[...]
    x_steps = [xpad_ref[j * SUBLANES:(j + 1) * SUBLANES, :] for j in range(CONV_W - 1)]
    for t0 in range(0, steps, sub_steps):
        for i in range(sub_steps):
            x_steps.append(jnp.concatenate(
                [xs_ref.at[s][pl.ds(t0 + i, n_seq, stride=LRU_PITCH), :] for s in range(slabs)], axis=1))
        xc = cb_ref[...]
        for j in range(CONV_W):
            xc = xc + jnp.concatenate(x_steps[j:j + sub_steps], axis=0) * cw_ref[j:j + 1, :]
        x_steps = x_steps[sub_steps:]
        a, u = _lru_gates(xc, w_ref, ba_ref, bx_ref, lam_ref)
        for i in range(sub_steps):
            g = i * SUBLANES
            h = a[g:g + SUBLANES, :] * h + u[g:g + SUBLANES, :]
            for s in range(slabs):
                hs_ref.at[s][pl.ds(t0 + i, n_seq, stride=LRU_PITCH), :] = h[:, s * LANES:(s + 1) * LANES]
    state_ref[...] = h
    for j in range(CONV_W - 1):
        xpad_ref[j * SUBLANES:(j + 1) * SUBLANES, :] = x_steps[j]

    for b in range(n_seq):
        plane = slice(b * LRU_PITCH, b * LRU_PITCH + steps)
        h_seq = jnp.concatenate([hs_ref[s, plane, :] for s in range(slabs)], axis=1)
        o_ref[b] = (jax.nn.gelu(y_ref[b]) * h_seq).astype(o_ref.dtype)

    @pl.when(blk == pl.num_programs(1) - 1)
    def _():
        hl_ref[...] = h
        xt_ref[...] = xpad_ref[...]


def _lru_specs(rows, index):
    return pl.BlockSpec((rows, LRU_TILE), index)


def _lru_prompt(h, w_xy, layer, conv_w, conv_b, w_bd, b_a, b_x, lam):
    batch, seq, d = h.shape
    assert batch == SUBLANES
    tail = (CONV_W - 1) * batch
    vec = lambda r: _lru_specs(r, lambda w, t: (0, w))
    staging = pltpu.VMEM((LRU_TILE // LANES, batch * LRU_PITCH, LANES), F32)
    return pl.pallas_call(
        _lru_prompt_body,
        grid=(LRU_NTILES, seq // LRU_STEPS),
        in_specs=[
            pl.BlockSpec((batch, LRU_STEPS, d), lambda w, t: (0, t, 0)),
            pl.BlockSpec((1, 1) + w_xy.shape[2:], lambda w, t: (layer, w, 0, 0)),
            vec(CONV_W), vec(1),
            pl.BlockSpec((1, 1) + w_bd.shape[2:], lambda w, t: (layer, w, 0, 0)),
            vec(1), vec(1), vec(1),
        ],
        out_specs=[pl.BlockSpec((batch, LRU_STEPS, LRU_TILE), lambda w, t: (0, t, w)), vec(batch), vec(tail)],
        out_shape=[
            jax.ShapeDtypeStruct((batch, seq, LRU_WIDTH), BF16),
            jax.ShapeDtypeStruct((batch, LRU_WIDTH), F32),
            jax.ShapeDtypeStruct((tail, LRU_WIDTH), F32),
        ],
        scratch_shapes=[
            staging, staging,
            pltpu.VMEM((batch, LRU_STEPS, LRU_TILE), F32),
            pltpu.VMEM((tail, LRU_TILE), F32),
            pltpu.VMEM((batch, LRU_TILE), F32),
        ],
        compiler_params=_params("parallel", "arbitrary"),
        name="lru_prompt",
    )(h, w_xy, conv_w, conv_b.reshape(1, -1), w_bd, b_a.reshape(1, -1), b_x.reshape(1, -1), lam.reshape(1, -1))


def _lru_step_body(h_ref, wxy_ref, b0_ref, b1_ref, b2_ref, h0_ref, cw_ref, cb_ref, w_ref, ba_ref, bx_ref,
                   lam_ref, o_ref, hn_ref, x_ref):
    tw = o_ref.shape[1]
    xy = jnp.dot(h_ref[...], wxy_ref[0, 0], preferred_element_type=F32)
    x = xy[:, :tw]
    x_ref[...] = x
    taps = (b0_ref[...], b1_ref[...], b2_ref[...], x)
    xc = cb_ref[...]
    for j in range(CONV_W):
        xc = xc + taps[j] * cw_ref[j:j + 1, :]
    a, u = _lru_gates(xc, w_ref, ba_ref, bx_ref, lam_ref)
    h = u + a * h0_ref[...]
    hn_ref[...] = h
    o_ref[...] = (jax.nn.gelu(xy[:, tw:]) * h).astype(o_ref.dtype)


def _lru_step(h, w_xy, layer, conv_buf, h0, conv_w, conv_b, w_bd, b_a, b_x, lam):
    batch, d = h.shape
    tile = lambda off: _lru_specs(batch, lambda w: (0, off + w))
    vec = lambda rows: _lru_specs(rows, lambda w: (0, w))
    f32_out = jax.ShapeDtypeStruct((batch, LRU_WIDTH), F32)
    return pl.pallas_call(
        _lru_step_body,
        grid=(LRU_NTILES,),
        in_specs=[
            pl.BlockSpec(h.shape, lambda w: (0, 0)),
            pl.BlockSpec((1, 1) + w_xy.shape[2:], lambda w: (layer, w, 0, 0)),
            tile(0), tile(LRU_NTILES), tile(2 * LRU_NTILES),
            tile(0),
            vec(CONV_W), vec(1),
            pl.BlockSpec((1, 1) + w_bd.shape[2:], lambda w: (layer, w, 0, 0)),
            vec(1), vec(1), vec(1),
        ],
        out_specs=[tile(0), tile(0), tile(0)],
        out_shape=[jax.ShapeDtypeStruct((batch, LRU_WIDTH), BF16), f32_out, f32_out],
        compiler_params=_params("parallel"),
        name="lru_step",
    )(h, w_xy, conv_buf, conv_buf, conv_buf, h0, conv_w, conv_b.reshape(1, -1), w_bd,
      b_a.reshape(1, -1), b_x.reshape(1, -1), lam.reshape(1, -1))


def _merge_body(ret_ref, lru_ref, h_ref, x_ref, wr_ref, wl_ref, wg_ref, wo_ref, bg_ref, o_ref):
    h = h_ref[...]
    gate_a = jax.nn.sigmoid(jnp.dot(h, wg_ref[0, :, :D_MODEL], preferred_element_type=F32) + bg_ref[0:1, :])
    merged = gate_a * jnp.dot(ret_ref[...], wr_ref[0], preferred_element_type=F32)
    gate_b = jax.nn.sigmoid(jnp.dot(h, wg_ref[0, :, D_MODEL:], preferred_element_type=F32) + bg_ref[1:2, :])
    merged = merged + gate_b * jnp.dot(lru_ref[...], wl_ref[0], preferred_element_type=F32)
    o_ref[...] = x_ref[...] + jnp.dot(merged.astype(BF16), wo_ref[0], preferred_element_type=F32)


def _merge(ret_g, lru_g, h, x, w_ret, w_lru, w_gab, w_o, layer, b_gate, tm):
    t = x.shape[0]
    rows = lambda n: pl.BlockSpec((tm, n), lambda i: (i, 0))
    whole = lambda w: _layer_block(w.shape, layer)
    return pl.pallas_call(
        _merge_body,
        grid=(t // tm,),
        in_specs=[rows(RET_V), rows(LRU_WIDTH), rows(D_MODEL), rows(D_MODEL),
                  whole(w_ret), whole(w_lru), whole(w_gab), whole(w_o), pl.BlockSpec(b_gate.shape, lambda i: (0, 0))],
        out_specs=rows(D_MODEL),
        out_shape=jax.ShapeDtypeStruct((t, D_MODEL), F32),
        compiler_params=_params("parallel"),
        name="merge_out",
    )(ret_g, lru_g, h, x, w_ret, w_lru, w_gab, w_o, b_gate)


def _ffn_body(x_ref, g_ref, wgu_ref, wd_ref, gn_ref, *out_refs):
    x = x_ref[...]
    h = _rms(x, g_ref[...]).astype(BF16)
    acc = jnp.zeros(x.shape, F32)
    for c in range(0, D_FF, FFN_CHUNK):
        gate = jnp.dot(h, wgu_ref[0, :, c:c + FFN_CHUNK], preferred_element_type=F32)
        up = jnp.dot(h, wgu_ref[0, :, D_FF + c:D_FF + c + FFN_CHUNK], preferred_element_type=F32)
        act = (jax.nn.silu(gate) * up).astype(BF16)
        acc = acc + jnp.dot(act, wd_ref[0, c:c + FFN_CHUNK, :], preferred_element_type=F32)
    x_new = x + acc
    if len(out_refs) == 2:
        out_refs[0][...] = x_new
    out_refs[-1][...] = _rms(x_new, gn_ref[...]).astype(out_refs[-1].dtype)


def _ffn(x, gain, w_gate_up, w_down, layer, next_gain, last, tm):
    t = x.shape[0]
    rows = pl.BlockSpec((tm, D_MODEL), lambda i: (i, 0))
    vec = pl.BlockSpec((1, D_MODEL), lambda i: (0, 0))
    if last:
        out_specs, out_shape = [rows], [jax.ShapeDtypeStruct((t, D_MODEL), F32)]
    else:
        out_specs = [rows, rows]
        out_shape = [jax.ShapeDtypeStruct((t, D_MODEL), F32), jax.ShapeDtypeStruct((t, D_MODEL), BF16)]
    return pl.pallas_call(
        _ffn_body,
        grid=(t // tm,),
        in_specs=[rows, vec, _layer_block(w_gate_up.shape, layer), _layer_block(w_down.shape, layer), vec],
        out_specs=out_specs,
        out_shape=out_shape,
        compiler_params=_params("parallel"),
        name="swiglu",
    )(x, gain.reshape(1, D_MODEL), w_gate_up, w_down, next_gain.reshape(1, D_MODEL))


def _rope_tables(pos):
    freqs = ROPE_THETA ** (-jnp.arange(ROPE_HALF, dtype=F32) / ROPE_HALF)
    ang = pos[:, None] * freqs[None, :]
    return jnp.cos(ang), jnp.sin(ang)


def _block_diag_gates(w_a, w_x):
    depth = w_a.shape[0]
    per_tile = LRU_TILE // LRU_BLOCK
    eye = jnp.eye(per_tile, dtype=w_a.dtype)

    def dense(w):
        w = w.reshape(depth, LRU_NTILES, per_tile, LRU_BLOCK, LRU_BLOCK)
        return jnp.einsum("ltncd,nm->ltncmd", w, eye).reshape(depth, LRU_NTILES, LRU_TILE, LRU_TILE)

    return jnp.concatenate([dense(w_a), dense(w_x)], axis=-1).astype(BF16)


def kernel(x_prompt, x_sample, state_ret, state_lru, state_conv, w_in, norm_mix, gn_gain, w_ret_out, conv_w,
           conv_b, lru_wa, lru_ba, lru_wx, lru_bx, lru_lambda, w_lru_out, b_gate, w_o, norm_ffn, w_gate_up,
           w_down, norm_final):
    bp, lp, _ = x_prompt.shape
    bs, ls, _ = x_sample.shape
    assert ls == 1 and lp % RET_ROWS == 0 and lp % LRU_STEPS == 0 and bs % STEP_SEQS == 0
    depth = w_in.shape[0]

    w_in = w_in.astype(BF16)
    tile_cols = lambda lo: w_in[:, :, lo:lo + LRU_TILE]
    w_xy = jnp.stack([jnp.concatenate([tile_cols(COL_XY + w * LRU_TILE), tile_cols(COL_XY + LRU_WIDTH + w * LRU_TILE)],
                                      axis=-1) for w in range(LRU_NTILES)], axis=1)
    w_gab = w_in[:, :, COL_GAB:]
    w_ret_out, w_lru_out, w_o = w_ret_out.astype(BF16), w_lru_out.astype(BF16), w_o.astype(BF16)
    w_gate_up, w_down = w_gate_up.astype(BF16), w_down.astype(BF16)
    w_bd = _block_diag_gates(lru_wa, lru_wx)

    cos_p, sin_p = _rope_tables(jnp.arange(lp, dtype=F32))
    cos_s, sin_s = _rope_tables(PAST_LEN + jnp.arange(ls, dtype=F32))
    cos_s = jnp.broadcast_to(cos_s, (bs, ROPE_HALF))
    sin_s = jnp.broadcast_to(sin_s, (bs, ROPE_HALF))

    xp = x_prompt.reshape(bp * lp, D_MODEL)
    xs = x_sample.reshape(bs * ls, D_MODEL)
    hp = _rmsnorm(xp, norm_mix[0], BF16, 1024)
    hs = _rmsnorm(xs, norm_mix[0], BF16, bs)
    lru_p, conv_p, lru_s, conv_s = [], [], [], []
    ret_p = ret_s = None
    for l in range(depth):
        last = l == depth - 1
        next_gain = norm_final if last else norm_mix[l + 1]
        lru_args = (conv_w[l], conv_b[l], w_bd, lru_ba[l], lru_bx[l], lru_lambda[l])

        ret_g, ret_p = _retention_prompt(hp, w_in, cos_p, sin_p, gn_gain[l], bp, lp, l, ret_p)
        lru_g, h_last, x_tail = _lru_prompt(hp.reshape(bp, lp, D_MODEL), w_xy, l, *lru_args)
        lru_p.append(h_last)
        conv_p.append(x_tail.reshape(CONV_W - 1, bp, LRU_WIDTH).transpose(1, 0, 2))
        xp = _merge(ret_g, lru_g.reshape(bp * lp, LRU_WIDTH), hp, xp, w_ret_out, w_lru_out, w_gab, w_o, l,
                    b_gate[l], 512)
        out = _ffn(xp, norm_ffn[l], w_gate_up, w_down, l, next_gain, last, 512)
        xp, hp = (None, out[0]) if last else out

        qkv = _qkv_proj(hs, w_in, l, cos_s, sin_s)
        g_ret = _gate_proj(hs, w_in, l)
        ret_g, ret_s = _retention_step(qkv, g_ret, gn_gain[l], state_ret, l, ret_s)
        buf2d = state_conv[l].reshape(bs, (CONV_W - 1) * LRU_WIDTH)
        lru_g, h_new, x_lru = _lru_step(hs, w_xy, l, buf2d, state_lru[l], *lru_args)
        lru_s.append(h_new)
        conv_s.append(jnp.concatenate([state_conv[l][:, 1:], x_lru[:, None, :]], axis=1))
        xs = _merge(ret_g.astype(BF16), lru_g, hs, xs, w_ret_out, w_lru_out, w_gab, w_o, l, b_gate[l], bs)
        out = _ffn(xs, norm_ffn[l], w_gate_up, w_down, l, next_gain, last, bs)
        xs, hs = (None, out[0]) if last else out

    stack = lambda parts: jnp.stack(parts, 0)
    return (hp.reshape(bp, lp, D_MODEL), hs.reshape(bs, ls, D_MODEL), ret_p, stack(lru_p), stack(conv_p),
            ret_s, stack(lru_s), stack(conv_s))
```
